```python
import math
import jax, jax.numpy as jnp
from jax import lax
import numpy as np

D_MODEL = 2048
BATCH = 1
SEQ = 16384
DEPTH = 1

CHUNK = 64
PLE_DIM = 256
D_FF = 5632
DN_HEADS = 16
DN_HEAD_DIM = 128
DN_WIDTH = DN_HEADS * DN_HEAD_DIM
DN_CONV = 4
SB_HEADS = 16
SB_HEAD_DIM = 128
SB_WIDTH = SB_HEADS * SB_HEAD_DIM
SB_BLOCK = 128
IN_WIDTH = 4 * DN_WIDTH + 2 * DN_HEADS + 3 * SB_WIDTH + 2 * D_MODEL
NORM_EPS = 1e-6
L2_EPS = 1e-6

kernel_name = "hybrid_deltanet_stickbreaking_macaron_block"


def _rms_norm(x, g):
    xf = x.astype(jnp.float32)
    y = xf * lax.rsqrt(jnp.mean(xf * xf, axis=-1, keepdims=True) + NORM_EPS)
    return (y * g.astype(jnp.float32)).astype(x.dtype)


def _l2norm(t):
    tf = t.astype(jnp.float32)
    return tf * lax.rsqrt(jnp.sum(tf * tf, axis=-1, keepdims=True) + L2_EPS)


def _swiglu(u, w_gate, w_up, w_down):
    return (jax.nn.silu(u @ w_gate) * (u @ w_up)) @ w_down


def _causal_dwconv(x, w):
    K, C = w.shape
    return lax.conv_general_dilated(
        x, w[:, None, :].astype(x.dtype), window_strides=(1,), padding=[(K - 1, 0)],
        dimension_numbers=("NWC", "WIO", "NWC"), feature_group_count=C)


def _gated_delta_rule(q, k, v, g, beta):
    B, S, H, dk = q.shape
    dv = v.shape[-1]
    C = CHUNK
    N = S // C

    def chunks(t):
        return t.reshape((B, N, C, H) + t.shape[3:]).swapaxes(2, 3)

    q = chunks(q.astype(jnp.float32)) * (dk ** -0.5)
    k = chunks(k.astype(jnp.float32))
    v = chunks(v.astype(jnp.float32))
    g = chunks(g.astype(jnp.float32))
    beta = chunks(beta.astype(jnp.float32))

    gc = jnp.cumsum(g, axis=-1)
    idx = jnp.arange(C)
    lower = idx[:, None] >= idx[None, :]
    strict = idx[:, None] > idx[None, :]
    decay = jnp.exp(jnp.where(lower, gc[..., :, None] - gc[..., None, :], -jnp.inf))

    kb = k * beta[..., None]
    M = jnp.where(strict, jnp.einsum("bnhcd,bnhed->bnhce", kb, k) * decay, 0.0)
    lhs = M + jnp.eye(C, dtype=jnp.float32)
    rhs = jnp.concatenate([v * beta[..., None], kb * jnp.exp(gc)[..., None]], axis=-1)
    sol = lax.linalg.triangular_solve(lhs, rhs, left_side=True, lower=True)
    u, w = sol[..., :dv], sol[..., dv:]

    attn = jnp.einsum("bnhcd,bnhed->bnhce", q, k) * decay
    qg = q * jnp.exp(gc)[..., None]
    kg = k * jnp.exp(gc[..., -1:] - gc)[..., None]
    glast = jnp.exp(gc[..., -1])

    def step(state, xs):
        u_c, w_c, attn_c, qg_c, kg_c, gl_c = xs
        v_new = u_c - jnp.einsum("bhcd,bhde->bhce", w_c, state)
        o = jnp.einsum("bhcd,bhde->bhce", qg_c, state) + jnp.einsum("bhce,bhef->bhcf", attn_c, v_new)
        state = state * gl_c[..., None, None] + jnp.einsum("bhcd,bhce->bhde", kg_c, v_new)
        return state, o

    xs = tuple(jnp.moveaxis(t, 1, 0) for t in (u, w, attn, qg, kg, glast))
    s0 = jnp.zeros((B, H, dk, dv), jnp.float32)
    _, o = lax.scan(step, s0, xs)
    return o.transpose(1, 0, 3, 2, 4).reshape(B, S, H, dv)


def _stick_breaking(q, k, v):
    B, S, H, Dh = q.shape
    nb = S // SB_BLOCK
    scale = Dh ** -0.5
    qb = q.reshape(B, nb, SB_BLOCK, H, Dh).transpose(1, 0, 3, 2, 4)
    kpos = jnp.arange(S)

    def block(args):
        qi, bi = args
        z = jnp.einsum("bhqd,bkhd->bhqk", qi, k, preferred_element_type=jnp.float32) * scale
        qpos = bi * SB_BLOCK + jnp.arange(SB_BLOCK)
        mask = kpos[None, :] < qpos[:, None]
        ls = jax.nn.log_sigmoid(z)
        l1m = jnp.where(mask, ls - z, 0.0)
        between = lax.cumsum(l1m, axis=3, reverse=True) - l1m
        A = jnp.where(mask, jnp.exp(ls + between), 0.0)
        return jnp.einsum("bhqk,bkhd->bqhd", A, v.astype(jnp.float32)).astype(q.dtype)

    out = lax.map(block, (qb, jnp.arange(nb)))
    return out.transpose(1, 0, 2, 3, 4).reshape(B, S, H * Dh)


def _mixer(u, w_in, dn_conv_w, dn_A_log, dn_dt_bias, dn_out_norm, w_branch_dn, w_branch_sb, w_out):
    B, S, _ = u.shape
    proj = u @ w_in
    o1 = 3 * DN_WIDTH
    o2 = o1 + DN_WIDTH
    o3 = o2 + DN_HEADS
    o4 = o3 + DN_HEADS
    o5 = o4 + 3 * SB_WIDTH
    o6 = o5 + D_MODEL
    dn_qkv, z, b, a = proj[..., :o1], proj[..., o1:o2], proj[..., o2:o3], proj[..., o3:o4]
    sb_qkv, gate_dn, gate_sb = proj[..., o4:o5], proj[..., o5:o6], proj[..., o6:]

    dn_qkv = jax.nn.silu(_causal_dwconv(dn_qkv, dn_conv_w))
    dq, dk, dv = jnp.split(dn_qkv, 3, axis=-1)
    dq = _l2norm(dq.reshape(B, S, DN_HEADS, DN_HEAD_DIM))
    dk = _l2norm(dk.reshape(B, S, DN_HEADS, DN_HEAD_DIM))
    dv = dv.reshape(B, S, DN_HEADS, DN_HEAD_DIM)
    beta = jax.nn.sigmoid(b.astype(jnp.float32))
    g = -jnp.exp(dn_A_log.astype(jnp.float32)) * jax.nn.softplus(
        a.astype(jnp.float32) + dn_dt_bias.astype(jnp.float32))
    o_dn = _gated_delta_rule(dq, dk, dv, g, beta)
    zf = z.reshape(B, S, DN_HEADS, DN_HEAD_DIM).astype(jnp.float32)
    o_dn = (_rms_norm(o_dn, dn_out_norm) * jax.nn.silu(zf)).astype(u.dtype).reshape(B, S, DN_WIDTH)

    sq, sk, sv = jnp.split(sb_qkv, 3, axis=-1)
    heads = lambda t: t.reshape(B, S, SB_HEADS, SB_HEAD_DIM)
    o_sb = _stick_breaking(heads(sq), heads(sk), heads(sv))

    merged = (jax.nn.sigmoid(gate_dn) * (o_dn @ w_branch_dn)
              + jax.nn.sigmoid(gate_sb) * (o_sb @ w_branch_sb))
    return merged @ w_out


def setup_inputs(seed: int = 0) -> dict:
    key = jax.random.key(seed)
    ks = jax.random.split(key, 32)
    L, D = DEPTH, D_MODEL

    def w(k, shape, fan_in):
        return jax.random.normal(k, shape, jnp.float32) * (fan_in ** -0.5)

    def gain(k, n):
        return 1.0 + 0.02 * jax.random.normal(k, (L, n), jnp.float32)

    dt = jnp.exp(jax.random.uniform(ks[20], (L, DN_HEADS), jnp.float32,
                                    minval=math.log(1e-3), maxval=math.log(1e-1)))
    return {
        "x": jax.random.normal(ks[0], (BATCH, SEQ, D), jnp.float32),
        "p": jax.random.normal(ks[1], (L, BATCH, SEQ, PLE_DIM), jnp.float32),
        "ffn1_norm_pre": gain(ks[2], D),
        "ffn1_w_gate": w(ks[3], (L, D, D_FF), D),
        "ffn1_w_up": w(ks[4], (L, D, D_FF), D),
        "ffn1_w_down": w(ks[5], (L, D_FF, D), D_FF),
        "ffn1_norm_post": gain(ks[6], D),
        "mix_norm_pre": gain(ks[7], D),
        "w_in": w(ks[8], (L, D, IN_WIDTH), D),
        "dn_conv_w": w(ks[9], (L, DN_CONV, 3 * DN_WIDTH), DN_CONV),
        "dn_A_log": jnp.log(jax.random.uniform(ks[10], (L, DN_HEADS), jnp.float32, minval=1.0, maxval=16.0)),
        "dn_dt_bias": dt + jnp.log(-jnp.expm1(-dt)),
        "dn_out_norm": gain(ks[11], DN_HEAD_DIM),
        "w_branch_dn": w(ks[12], (L, DN_WIDTH, D), DN_WIDTH),
        "w_branch_sb": w(ks[13], (L, SB_WIDTH, D), SB_WIDTH),
        "w_out": w(ks[14], (L, D, D), D),
        "mix_norm_post": gain(ks[15], D),
        "ffn2_norm_pre": gain(ks[16], D),
        "ffn2_w_gate": w(ks[17], (L, D, D_FF), D),
        "ffn2_w_up": w(ks[18], (L, D, D_FF), D),
        "ffn2_w_down": w(ks[19], (L, D_FF, D), D_FF),
        "ffn2_norm_post": gain(ks[21], D),
        "ple_norm_pre": gain(ks[22], D),
        "ple_w_gate": w(ks[23], (L, D, D), D),
        "ple_w_proj": w(ks[24], (L, PLE_DIM, D), PLE_DIM),
        "ple_norm_post": gain(ks[25], D),
    }


def reference(x, p, ffn1_norm_pre, ffn1_w_gate, ffn1_w_up, ffn1_w_down, ffn1_norm_post,
              mix_norm_pre, w_in, dn_conv_w, dn_A_log, dn_dt_bias, dn_out_norm,
              w_branch_dn, w_branch_sb, w_out, mix_norm_post,
              ffn2_norm_pre, ffn2_w_gate, ffn2_w_up, ffn2_w_down, ffn2_norm_post,
              ple_norm_pre, ple_w_gate, ple_w_proj, ple_norm_post):
    h = x
    for i in range(DEPTH):
        f = _swiglu(_rms_norm(h, ffn1_norm_pre[i]), ffn1_w_gate[i], ffn1_w_up[i], ffn1_w_down[i])
        h = h + 0.5 * _rms_norm(f, ffn1_norm_post[i])
        m = _mixer(_rms_norm(h, mix_norm_pre[i]), w_in[i], dn_conv_w[i], dn_A_log[i], dn_dt_bias[i],
                   dn_out_norm[i], w_branch_dn[i], w_branch_sb[i], w_out[i])
        h = h + _rms_norm(m, mix_norm_post[i])
        f = _swiglu(_rms_norm(h, ffn2_norm_pre[i]), ffn2_w_gate[i], ffn2_w_up[i], ffn2_w_down[i])
        h = h + 0.5 * _rms_norm(f, ffn2_norm_post[i])
        gate = jax.nn.sigmoid(_rms_norm(h, ple_norm_pre[i]) @ ple_w_gate[i])
        h = h + _rms_norm(gate * (p[i] @ ple_w_proj[i]), ple_norm_post[i])
    return h
```

```python
import functools

import jax
import jax.numpy as jnp
from jax import lax
from jax.experimental import pallas as pl
from jax.experimental.pallas import tpu as pltpu

NORM_EPS = 1e-6
L2_EPS = 1e-6
CHUNK = 64
DN_HEADS = 16
SB_HEADS = 16
HEAD_DIM = 128
DN_CONV = 4
LANE = 128
SB_SKIP_LOG = -104.0
VMEM_LIMIT = 56 * 1024 * 1024

F32 = jnp.float32
BF16 = jnp.bfloat16
HI = lax.Precision.HIGHEST


def _cparams(*sem):
    return pltpu.CompilerParams(dimension_semantics=sem, vmem_limit_bytes=VMEM_LIMIT)


def _rms(x, g):
    return x * lax.rsqrt(jnp.mean(x * x, axis=-1, keepdims=True) + NORM_EPS) * g


def _sigmoid(x):
    return 1.0 / (1.0 + jnp.exp(-x))


def _silu(x):
    return x * _sigmoid(x)


def _softplus(x):
    return jnp.maximum(x, 0.0) + jnp.log1p(jnp.exp(-jnp.abs(x)))


def _dot(a, b):
    return jnp.dot(a, b, preferred_element_type=F32)


def _dot_nt(a, b):
    return lax.dot_general(a, b, (((1,), (1,)), ((), ())), preferred_element_type=F32)


def _dot_tn(a, b):
    return lax.dot_general(a, b, (((0,), (0,)), ((), ())), preferred_element_type=F32)


def _dot_hi(a, b):
    return jnp.dot(a, b, preferred_element_type=F32, precision=HI)


def _ffn_kernel(h_ref, gpre_ref, wg_ref, wu_ref, wd_ref, gpost_ref, o_ref, u_ref, acc_ref):
    j = pl.program_id(1)

    @pl.when(j == 0)
    def _():
        u_ref[...] = _rms(h_ref[...], gpre_ref[...]).astype(BF16)
        acc_ref[...] = jnp.zeros_like(acc_ref)

    u = u_ref[...]
    g = _dot(u, wg_ref[...])
    up = _dot(u, wu_ref[...])
    a = (_silu(g) * up).astype(BF16)
    acc_ref[...] += _dot(a, wd_ref[...])

    @pl.when(j == pl.num_programs(1) - 1)
    def _():
        o_ref[...] = h_ref[...] + 0.5 * _rms(acc_ref[...], gpost_ref[...])


def _ffn(h, gpre, wg, wu, wd, gpost, tm, tf):
    S, D = h.shape
    FF = wg.shape[1]
    return pl.pallas_call(
        _ffn_kernel,
        grid=(S // tm, FF // tf),
        in_specs=[
            pl.BlockSpec((tm, D), lambda i, j: (i, 0)),
            pl.BlockSpec((1, D), lambda i, j: (0, 0)),
            pl.BlockSpec((D, tf), lambda i, j: (0, j)),
            pl.BlockSpec((D, tf), lambda i, j: (0, j)),
            pl.BlockSpec((tf, D), lambda i, j: (j, 0)),
            pl.BlockSpec((1, D), lambda i, j: (0, 0)),
        ],
        out_specs=pl.BlockSpec((tm, D), lambda i, j: (i, 0)),
        out_shape=jax.ShapeDtypeStruct((S, D), F32),
        scratch_shapes=[pltpu.VMEM((tm, D), BF16), pltpu.VMEM((tm, D), F32)],
        compiler_params=_cparams("parallel", "arbitrary"),
        name="ffn",
    )(h, gpre, wg, wu, wd, gpost)


def _proj_heads_kernel(h_ref, g_ref, w_ref, o_ref, u_ref):
    @pl.when(pl.program_id(1) == 0)
    def _():
        u_ref[...] = _rms(h_ref[...], g_ref[...]).astype(BF16)

    res = _dot(u_ref[...], w_ref[...])
    for c in range(o_ref.shape[0]):
        o_ref[c] = res[:, c * LANE:(c + 1) * LANE].astype(o_ref.dtype)


def _proj_heads(h, g, w, dtype, tm, tn):
    S, D = h.shape
    N = w.shape[1]
    return pl.pallas_call(
        _proj_heads_kernel,
        grid=(S // tm, N // tn),
        in_specs=[
            pl.BlockSpec((tm, D), lambda i, j: (i, 0)),
            pl.BlockSpec((1, D), lambda i, j: (0, 0)),
            pl.BlockSpec((D, tn), lambda i, j: (0, j)),
        ],
        out_specs=pl.BlockSpec((tn // LANE, tm, LANE), lambda i, j: (j, i, 0)),
        out_shape=jax.ShapeDtypeStruct((N // LANE, S, LANE), dtype),
        scratch_shapes=[pltpu.VMEM((tm, D), BF16)],
        compiler_params=_cparams("parallel", "arbitrary"),
        name="proj_heads",
    )(h, g, w)


def _proj_rows_kernel(h_ref, g_ref, w_ref, wt_ref, o_ref, ot_ref, u_ref):
    j = pl.program_id(1)

    @pl.when(j == 0)
    def _():
        u_ref[...] = _rms(h_ref[...], g_ref[...]).astype(BF16)
        ot_ref[...] = _dot_nt(wt_ref[...], u_ref[...])

    o_ref[...] = _dot(u_ref[...], w_ref[...])


def _proj_rows(h, g, w, wt, tm, tn):
    S, D = h.shape
    N = w.shape[1]
    R = wt.shape[0]
    return pl.pallas_call(
        _proj_rows_kernel,
        grid=(S // tm, N // tn),
        in_specs=[
            pl.BlockSpec((tm, D), lambda i, j: (i, 0)),
            pl.BlockSpec((1, D), lambda i, j: (0, 0)),
            pl.BlockSpec((D, tn), lambda i, j: (0, j)),
            pl.BlockSpec((R, D), lambda i, j: (0, 0)),
        ],
        out_specs=[
            pl.BlockSpec((tm, tn), lambda i, j: (i, j)),
            pl.BlockSpec((R, tm), lambda i, j: (0, i)),
        ],
        out_shape=[jax.ShapeDtypeStruct((S, N), F32), jax.ShapeDtypeStruct((R, S), F32)],
        scratch_shapes=[pltpu.VMEM((tm, D), BF16)],
        compiler_params=_cparams("parallel", "arbitrary"),
        name="proj_rows",
    )(h, g, w, wt)


def _dn_kernel(alog_ref, dtb_ref,
               q_ref, k_ref, v_ref, z_ref, b_ref, a_ref,
               cw_ref, gn_ref, o_ref,
               state_ref, tail_ref, xbuf_ref, q_s, k_s, v_s):
    h = pl.program_id(0)
    TB = q_ref.shape[1]
    C = CHUNK
    NC = TB // C

    @pl.when(pl.program_id(1) == 0)
    def _():
        state_ref[...] = jnp.zeros_like(state_ref)
        tail_ref[...] = jnp.zeros_like(tail_ref)

    def conv_silu(x_ref, idx):
        w = cw_ref[0, idx]
        xbuf_ref[0:8, :] = tail_ref[idx]
        xbuf_ref[8:8 + TB, :] = x_ref[0]
        tail_ref[idx] = x_ref[0, TB - 8:TB, :]
        y = (w[0:1, :] * xbuf_ref[5:5 + TB, :] + w[1:2, :] * xbuf_ref[6:6 + TB, :]
             + w[2:3, :] * xbuf_ref[7:7 + TB, :] + w[3:4, :] * xbuf_ref[8:8 + TB, :])
        return _silu(y)

    def l2n(t):
        return t * lax.rsqrt(jnp.sum(t * t, axis=-1, keepdims=True) + L2_EPS)

    q_s[...] = l2n(conv_silu(q_ref, 0)) * (HEAD_DIM ** -0.5)
    k_s[...] = l2n(conv_silu(k_ref, 1))
    v_s[...] = conv_silu(v_ref, 2)

    neg_a = -jnp.exp(alog_ref[h])
    dtb = dtb_ref[h]

    ri = lax.broadcasted_iota(jnp.int32, (C, C), 0)
    ci = lax.broadcasted_iota(jnp.int32, (C, C), 1)
    lower = ri >= ci
    strict = ri > ci
    eye = ri == ci
    upper_f = (ri <= ci).astype(F32)
    eye_f = eye.astype(F32)
    rhs2 = jnp.concatenate([upper_f, jnp.ones((C, LANE), F32)], axis=1)
    ones_l = jnp.ones((C, LANE), F32)
    gn = gn_ref[...]

    def chunk(c, carry):
        r0 = pl.multiple_of(c * C, C)
        b_row = b_ref[0, pl.ds(c, 1), :]
        a_row = a_ref[0, pl.ds(c, 1), :]
        beta_row = _sigmoid(b_row)
        g_row = neg_a * _softplus(a_row + dtb)
        g_b = jnp.broadcast_to(g_row, (C, C))
        beta_b = jnp.broadcast_to(beta_row, (C, C))
        lhs1 = jnp.concatenate([jnp.where(lower, g_b, 0.0), jnp.where(eye, beta_b, 0.0)], axis=0)
        r1 = _dot_hi(lhs1, ones_l)
        gccol = r1[:C]
        betacol = r1[C:]
        r2 = _dot_hi(g_b, rhs2)
        gcrow = r2[:, :C]
        glast = r2[:, C:]
        dd = jnp.where(lower, gccol[:, :C] - gcrow, 0.0)
        decay = jnp.where(lower, jnp.exp(dd), 0.0)

        qc = q_s[pl.ds(r0, C), :]
        kc = k_s[pl.ds(r0, C), :]
        vc = v_s[pl.ds(r0, C), :]
        kb = kc * betacol
        egc = jnp.exp(gccol)
        kc16 = kc.astype(BF16)
        pm = _dot_nt(kb.astype(BF16), kc16)
        x = -jnp.where(strict, pm * decay, 0.0)
        t = eye_f + x
        p = x
        for _ in range(5):
            p = _dot_hi(p, p)
            t = t + _dot_hi(t, p)
        rhs = jnp.concatenate([vc * betacol, kb * egc], axis=1)
        sol = _dot_hi(t, rhs)
        u = sol[:, :HEAD_DIM]
        w = sol[:, HEAD_DIM:]
        attn = jnp.where(lower, _dot_nt(qc.astype(BF16), kc16) * decay, 0.0)
        qg = qc * egc
        kg = kc * jnp.exp(glast - gccol)

        st = state_ref[...]
        st16 = st.astype(BF16)
        v_new = u - _dot(w.astype(BF16), st16)
        vn16 = v_new.astype(BF16)
        o = _dot(qg.astype(BF16), st16) + _dot(attn.astype(BF16), vn16)
        state_ref[...] = st * jnp.exp(glast[0:1, :]) + _dot_tn(kg.astype(BF16), vn16)

        zc = z_ref[0, pl.ds(r0, C), :]
        on = o * lax.rsqrt(jnp.mean(o * o, axis=-1, keepdims=True) + NORM_EPS) * gn
        o_ref[pl.ds(r0, C), :] = (on * _silu(zc)).astype(o_ref.dtype)
        return carry

    lax.fori_loop(0, NC, chunk, 0)


def _deltanet(dnz, ba, conv_w, a_log, dt_bias, gnorm, tb):
    H = DN_HEADS
    S = dnz.shape[1]
    NC = tb // CHUNK
    grid_spec = pltpu.PrefetchScalarGridSpec(
        num_scalar_prefetch=2,
        grid=(H, S // tb),
        in_specs=[
            pl.BlockSpec((1, tb, LANE), lambda h, t, *_: (h, t, 0)),
            pl.BlockSpec((1, tb, LANE), lambda h, t, *_: (H + h, t, 0)),
            pl.BlockSpec((1, tb, LANE), lambda h, t, *_: (2 * H + h, t, 0)),
            pl.BlockSpec((1, tb, LANE), lambda h, t, *_: (3 * H + h, t, 0)),
            pl.BlockSpec((1, NC, CHUNK), lambda h, t, *_: (h, t, 0)),
            pl.BlockSpec((1, NC, CHUNK), lambda h, t, *_: (H + h, t, 0)),
            pl.BlockSpec((1, 3, DN_CONV, LANE), lambda h, t, *_: (h, 0, 0, 0)),
            pl.BlockSpec((1, LANE), lambda h, t, *_: (0, 0)),
        ],
        out_specs=pl.BlockSpec((tb, LANE), lambda h, t, *_: (t, h)),
        scratch_shapes=[
            pltpu.VMEM((HEAD_DIM, HEAD_DIM), F32),
            pltpu.VMEM((3, 8, LANE), F32),
            pltpu.VMEM((tb + 8, LANE), F32),
            pltpu.VMEM((tb, LANE), F32),
            pltpu.VMEM((tb, LANE), F32),
            pltpu.VMEM((tb, LANE), F32),
        ],
    )
    return pl.pallas_call(
        _dn_kernel,
        grid_spec=grid_spec,
        out_shape=jax.ShapeDtypeStruct((S, H * HEAD_DIM), BF16),
        compiler_params=_cparams("parallel", "arbitrary"),
        name="deltanet",
    )(a_log, dt_bias, dnz, dnz, dnz, dnz, ba, ba, conv_w, gnorm)


def _sb_kernel(q_ref, k_ref, v_ref, o_ref):
    qi = pl.program_id(1)
    TQ = q_ref.shape[1]
    q = q_ref[0]
    scale = HEAD_DIM ** -0.5
    rq = lax.broadcasted_iota(jnp.int32, (TQ, TQ), 0)
    ck = lax.broadcasted_iota(jnp.int32, (TQ, TQ), 1)
    later = (rq > ck).astype(BF16)

    def cond(carry):
        j, cmax, _, _ = carry
        return jnp.logical_and(j >= 0, cmax > SB_SKIP_LOG)

    def body(carry):
        j, _, c, acc = carry
        k0 = pl.multiple_of(j * TQ, TQ)
        kj = k_ref[0, pl.ds(k0, TQ), :]
        vj = v_ref[0, pl.ds(k0, TQ), :]
        z = _dot_nt(q, kj) * scale
        mask = (ck + j * TQ) < (rq + qi * TQ)
        ls = jnp.minimum(z, 0.0) - jnp.log1p(jnp.exp(-jnp.abs(z)))
        l1m = jnp.where(mask, ls - z, 0.0)
        p1 = l1m.astype(BF16)
        r1 = l1m - p1.astype(F32)
        p2 = r1.astype(BF16)
        p3 = (r1 - p2.astype(F32)).astype(BF16)
        between = _dot(p1, later) + _dot(p2, later) + _dot(p3, later)
        a = jnp.where(mask, jnp.exp(ls + between + c), 0.0)
        acc = acc + _dot(a.astype(BF16), vj)
        c = c + jnp.sum(l1m, axis=1, keepdims=True)
        return j - 1, jnp.max(c), c, acc

    init = (qi, jnp.float32(0.0), jnp.zeros((TQ, 1), F32), jnp.zeros((TQ, HEAD_DIM), F32))
    _, _, _, acc = lax.while_loop(cond, body, init)
    o_ref[...] = acc.astype(o_ref.dtype)


def _stick_breaking(sb, tq):
    H = SB_HEADS
    S = sb.shape[1]
    return pl.pallas_call(
        _sb_kernel,
        grid=(H, S // tq),
        in_specs=[
            pl.BlockSpec((1, tq, LANE), lambda h, i: (h, i, 0)),
            pl.BlockSpec((1, S, LANE), lambda h, i: (H + h, 0, 0)),
            pl.BlockSpec((1, S, LANE), lambda h, i: (2 * H + h, 0, 0)),
        ],
        out_specs=pl.BlockSpec((tq, LANE), lambda h, i: (i, h)),
        out_shape=jax.ShapeDtypeStruct((S, H * HEAD_DIM), BF16),
        compiler_params=_cparams("parallel", "arbitrary"),
        name="stick_breaking",
    )(sb, sb, sb)


def _merge_kernel(od_ref, os_ref, gd_ref, gs_ref, wd_ref, ws_ref, o_ref):
    m = (_sigmoid(gd_ref[...]) * _dot(od_ref[...], wd_ref[...])
         + _sigmoid(gs_ref[...]) * _dot(os_ref[...], ws_ref[...]))
    o_ref[...] = m.astype(o_ref.dtype)


def _merge(o_dn, o_sb, gates, w_dn, w_sb, tm, tn):
    S, W = o_dn.shape
    D = w_dn.shape[1]
    nb = D // tn
    return pl.pallas_call(
        _merge_kernel,
        grid=(S // tm, nb),
        in_specs=[
            pl.BlockSpec((tm, W), lambda i, j: (i, 0)),
            pl.BlockSpec((tm, W), lambda i, j: (i, 0)),
            pl.BlockSpec((tm, tn), lambda i, j: (i, j)),
            pl.BlockSpec((tm, tn), lambda i, j: (i, nb + j)),
            pl.BlockSpec((W, tn), lambda i, j: (0, j)),
            pl.BlockSpec((W, tn), lambda i, j: (0, j)),
        ],
        out_specs=pl.BlockSpec((tm, tn), lambda i, j: (i, j)),
        out_shape=jax.ShapeDtypeStruct((S, D), BF16),
        compiler_params=_cparams("parallel", "arbitrary"),
        name="merge",
    )(o_dn, o_sb, gates, gates, w_dn, w_sb)


def _outproj_kernel(m_ref, w_ref, h_ref, g_ref, o_ref):
    o_ref[...] = h_ref[...] + _rms(_dot(m_ref[...], w_ref[...]), g_ref[...])


def _outproj(m, w, h, g, tm):
    S, D = h.shape
    return pl.pallas_call(
        _outproj_kernel,
        grid=(S // tm,),
        in_specs=[
            pl.BlockSpec((tm, D), lambda i: (i, 0)),
            pl.BlockSpec((D, D), lambda i: (0, 0)),
            pl.BlockSpec((tm, D), lambda i: (i, 0)),
            pl.BlockSpec((1, D), lambda i: (0, 0)),
        ],
        out_specs=pl.BlockSpec((tm, D), lambda i: (i, 0)),
        out_shape=jax.ShapeDtypeStruct((S, D), F32),
        compiler_params=_cparams("parallel"),
        name="outproj",
    )(m, w, h, g)


def _ple_kernel(h_ref, p_ref, gpre_ref, wg_ref, wp_ref, gpost_ref, o_ref):
    h = h_ref[...]
    u = _rms(h, gpre_ref[...]).astype(BF16)
    gate = _sigmoid(_dot(u, wg_ref[...]))
    e = _dot(p_ref[...].astype(BF16), wp_ref[...])
    o_ref[...] = h + _rms(gate * e, gpost_ref[...])


def _ple(h, p, gpre, wg, wp, gpost, tm):
    S, D = h.shape
    P = p.shape[1]
    return pl.pallas_call(
        _ple_kernel,
        grid=(S // tm,),
        in_specs=[
            pl.BlockSpec((tm, D), lambda i: (i, 0)),
            pl.BlockSpec((tm, P), lambda i: (i, 0)),
            pl.BlockSpec((1, D), lambda i: (0, 0)),
            pl.BlockSpec((D, D), lambda i: (0, 0)),
            pl.BlockSpec((P, D), lambda i: (0, 0)),
            pl.BlockSpec((1, D), lambda i: (0, 0)),
        ],
        out_specs=pl.BlockSpec((tm, D), lambda i: (i, 0)),
        out_shape=jax.ShapeDtypeStruct((S, D), F32),
        compiler_params=_cparams("parallel"),
        name="ple",
    )(h, p, gpre, wg, wp, gpost)


def _layer(h, p, ffn1_norm_pre, ffn1_w_gate, ffn1_w_up, ffn1_w_down, ffn1_norm_post,
           mix_norm_pre, w_in, dn_conv_w, dn_A_log, dn_dt_bias, dn_out_norm,
           w_branch_dn, w_branch_sb, w_out, mix_norm_post,
           ffn2_norm_pre, ffn2_w_gate, ffn2_w_up, ffn2_w_down, ffn2_norm_post,
           ple_norm_pre, ple_w_gate, ple_w_proj, ple_norm_post):
    S, D = h.shape
    H = DN_HEADS
    dn_w = H * HEAD_DIM
    sb_w = SB_HEADS * HEAD_DIM
    row = lambda g: g.reshape(1, -1).astype(F32)
    bf = lambda w: w.astype(BF16)
    tm = min(512, S)

    h = _ffn(h, row(ffn1_norm_pre), bf(ffn1_w_gate), bf(ffn1_w_up), bf(ffn1_w_down),
             row(ffn1_norm_post), tm, 512)

    o2 = 4 * dn_w
    o4 = o2 + 2 * H
    o5 = o4 + 3 * sb_w
    gpre = row(mix_norm_pre)
    tmp = min(1024, S)
    dnz = _proj_heads(h, gpre, bf(w_in[:, :o2]), F32, tmp, 1024)
    sb = _proj_heads(h, gpre, bf(w_in[:, o4:o5]), BF16, tmp, 1024)
    gates, ba = _proj_rows(h, gpre, bf(w_in[:, o5:]), bf(w_in[:, o2:o4].T), tmp, 1024)

    conv_w = dn_conv_w.reshape(DN_CONV, 3, H, HEAD_DIM).transpose(2, 1, 0, 3)
    o_dn = _deltanet(dnz, ba.reshape(2 * H, S // CHUNK, CHUNK), conv_w,
                     dn_A_log.astype(F32), dn_dt_bias.astype(F32), row(dn_out_norm), min(512, S))
    o_sb = _stick_breaking(sb, min(256, S))

    merged = _merge(o_dn, o_sb, gates, bf(w_branch_dn), bf(w_branch_sb), tm, 1024)
    h = _outproj(merged, bf(w_out), h, row(mix_norm_post), tm)

    h = _ffn(h, row(ffn2_norm_pre), bf(ffn2_w_gate), bf(ffn2_w_up), bf(ffn2_w_down),
             row(ffn2_norm_post), tm, 512)

    return _ple(h, p, row(ple_norm_pre), bf(ple_w_gate), bf(ple_w_proj), row(ple_norm_post), tm)


def kernel(x, p, ffn1_norm_pre, ffn1_w_gate, ffn1_w_up, ffn1_w_down, ffn1_norm_post, mix_norm_pre, w_in, dn_conv_w, dn_A_log, dn_dt_bias, dn_out_norm, w_branch_dn, w_branch_sb, w_out, mix_norm_post, ffn2_norm_pre, ffn2_w_gate, ffn2_w_up, ffn2_w_down, ffn2_norm_post, ple_norm_pre, ple_w_gate, ple_w_proj, ple_norm_post):
    B = x.shape[0]
    depth = w_in.shape[0]
    outs = []
    for b in range(B):
        h = x[b]
        for i in range(depth):
            h = _layer(h, p[i, b], ffn1_norm_pre[i], ffn1_w_gate[i], ffn1_w_up[i], ffn1_w_down[i],
                       ffn1_norm_post[i], mix_norm_pre[i], w_in[i], dn_conv_w[i], dn_A_log[i],
                       dn_dt_bias[i], dn_out_norm[i], w_branch_dn[i], w_branch_sb[i], w_out[i],
                       mix_norm_post[i], ffn2_norm_pre[i], ffn2_w_gate[i], ffn2_w_up[i],
                       ffn2_w_down[i], ffn2_norm_post[i], ple_norm_pre[i], ple_w_gate[i],
                       ple_w_proj[i], ple_norm_post[i])
        outs.append(h)
    return jnp.stack(outs)
```

```python
import jax
import jax.numpy as jnp
from jax import lax
from jax.experimental import pallas as pl
from jax.experimental.pallas import tpu as pltpu

NORM_EPS = 1e-6
L2_EPS = 1e-6
CHUNK = 64
DN_HEADS = 16
SB_HEADS = 16
HEAD_DIM = 128
DN_CONV = 4
LANE = 128
SB_SKIP_LOG = -104.0
VMEM_LIMIT = 56 * 1024 * 1024

F32 = jnp.float32
BF16 = jnp.bfloat16


def _cparams(*sem):
    return pltpu.CompilerParams(dimension_semantics=sem, vmem_limit_bytes=VMEM_LIMIT)


def _rms(x, g):
    return x * lax.rsqrt(jnp.mean(x * x, axis=-1, keepdims=True) + NORM_EPS) * g


def _sigmoid(x):
    return 1.0 / (1.0 + jnp.exp(-x))


def _silu(x):
    return x * _sigmoid(x)


def _softplus(x):
    return jnp.maximum(x, 0.0) + jnp.log1p(jnp.exp(-jnp.abs(x)))


def _dot(a, b):
    return jnp.dot(a, b, preferred_element_type=F32)


def _dot_nt(a, b):
    return lax.dot_general(a, b, (((1,), (1,)), ((), ())), preferred_element_type=F32)


def _dot_tn(a, b):
    return lax.dot_general(a, b, (((0,), (0,)), ((), ())), preferred_element_type=F32)


def _split2(a):
    hi = a.astype(BF16)
    return hi, (a - hi.astype(F32)).astype(BF16)


def _split3(a):
    p1 = a.astype(BF16)
    r = a - p1.astype(F32)
    p2 = r.astype(BF16)
    return p1, p2, (r - p2.astype(F32)).astype(BF16)


def _dot_split(ah, al, bh, bl):
    return _dot(ah, bh) + (_dot(ah, bl) + _dot(al, bh))


def _dot_01(a, b01):
    p1, p2, p3 = _split3(a)
    return _dot(p1, b01) + (_dot(p2, b01) + _dot(p3, b01))


def _ffn_kernel(h_ref, gpre_ref, wg_ref, wu_ref, wd_ref, gpost_ref, o_ref, u_ref, acc_ref):
    j = pl.program_id(1)

    @pl.when(j == 0)
    def _():
        u_ref[...] = _rms(h_ref[...], gpre_ref[...]).astype(BF16)
        acc_ref[...] = jnp.zeros_like(acc_ref)

    u = u_ref[...]
    g = _dot(u, wg_ref[...])
    up = _dot(u, wu_ref[...])
    a = (_silu(g) * up).astype(BF16)
    acc_ref[...] += _dot(a, wd_ref[...])

    @pl.when(j == pl.num_programs(1) - 1)
    def _():
        o_ref[...] = h_ref[...] + 0.5 * _rms(acc_ref[...], gpost_ref[...])


def _ffn(h, gpre, wg, wu, wd, gpost, tm, tf):
    S, D = h.shape
    FF = wg.shape[1]
    return pl.pallas_call(
        _ffn_kernel,
        grid=(S // tm, FF // tf),
        in_specs=[
            pl.BlockSpec((tm, D), lambda i, j: (i, 0)),
            pl.BlockSpec((1, D), lambda i, j: (0, 0)),
            pl.BlockSpec((D, tf), lambda i, j: (0, j)),
            pl.BlockSpec((D, tf), lambda i, j: (0, j)),
            pl.BlockSpec((tf, D), lambda i, j: (j, 0)),
            pl.BlockSpec((1, D), lambda i, j: (0, 0)),
        ],
        out_specs=pl.BlockSpec((tm, D), lambda i, j: (i, 0)),
        out_shape=jax.ShapeDtypeStruct((S, D), F32),
        scratch_shapes=[pltpu.VMEM((tm, D), BF16), pltpu.VMEM((tm, D), F32)],
        compiler_params=_cparams("parallel", "arbitrary"),
        name="ffn",
    )(h, gpre, wg, wu, wd, gpost)


def _proj_heads_kernel(h_ref, g_ref, w_ref, o_ref, u_ref):
    @pl.when(pl.program_id(1) == 0)
    def _():
        u_ref[...] = _rms(h_ref[...], g_ref[...]).astype(BF16)

    res = _dot(u_ref[...], w_ref[...])
    for c in range(o_ref.shape[0]):
        o_ref[c] = res[:, c * LANE:(c + 1) * LANE].astype(o_ref.dtype)


def _proj_heads(h, g, w, dtype, tm, tn):
    S, D = h.shape
    N = w.shape[1]
    return pl.pallas_call(
        _proj_heads_kernel,
        grid=(S // tm, N // tn),
        in_specs=[
            pl.BlockSpec((tm, D), lambda i, j: (i, 0)),
            pl.BlockSpec((1, D), lambda i, j: (0, 0)),
            pl.BlockSpec((D, tn), lambda i, j: (0, j)),
        ],
        out_specs=pl.BlockSpec((tn // LANE, tm, LANE), lambda i, j: (j, i, 0)),
        out_shape=jax.ShapeDtypeStruct((N // LANE, S, LANE), dtype),
        scratch_shapes=[pltpu.VMEM((tm, D), BF16)],
        compiler_params=_cparams("parallel", "arbitrary"),
        name="proj_heads",
    )(h, g, w)


def _proj_rows_kernel(h_ref, g_ref, w_ref, wt_ref, o_ref, ot_ref, u_ref):
    j = pl.program_id(1)

    @pl.when(j == 0)
    def _():
        u_ref[...] = _rms(h_ref[...], g_ref[...]).astype(BF16)
        ot_ref[...] = _dot_nt(wt_ref[...], u_ref[...])

    o_ref[...] = _dot(u_ref[...], w_ref[...])


def _proj_rows(h, g, w, wt, tm, tn):
    S, D = h.shape
    N = w.shape[1]
    R = wt.shape[0]
    return pl.pallas_call(
        _proj_rows_kernel,
        grid=(S // tm, N // tn),
        in_specs=[
            pl.BlockSpec((tm, D), lambda i, j: (i, 0)),
            pl.BlockSpec((1, D), lambda i, j: (0, 0)),
            pl.BlockSpec((D, tn), lambda i, j: (0, j)),
            pl.BlockSpec((R, D), lambda i, j: (0, 0)),
        ],
        out_specs=[
            pl.BlockSpec((tm, tn), lambda i, j: (i, j)),
            pl.BlockSpec((R, tm), lambda i, j: (0, i)),
        ],
        out_shape=[jax.ShapeDtypeStruct((S, N), F32), jax.ShapeDtypeStruct((R, S), F32)],
        scratch_shapes=[pltpu.VMEM((tm, D), BF16)],
        compiler_params=_cparams("parallel", "arbitrary"),
        name="proj_rows",
    )(h, g, w, wt)


def _dn_kernel(alog_ref, dtb_ref,
               q_ref, k_ref, v_ref, z_ref, b_ref, a_ref,
               cw_ref, gn_ref, o_ref,
               state_ref, tail_ref, xbuf_ref, q_s, k_s, kb_s, rhs_s, qg_s, kg_s,
               decay_s, egl_s, qt_s, o0_s, kw_s, bb_s):
    HG, TB = q_ref.shape[0], q_ref.shape[1]
    C = CHUNK
    NC = TB // C
    h0 = pl.program_id(0) * HG

    @pl.when(pl.program_id(1) == 0)
    def _():
        state_ref[...] = jnp.zeros_like(state_ref)
        tail_ref[...] = jnp.zeros_like(tail_ref)

    ri = lax.broadcasted_iota(jnp.int32, (TB, C), 0) & (C - 1)
    ci = lax.broadcasted_iota(jnp.int32, (TB, C), 1)
    lower = ri >= ci
    ri1 = lax.broadcasted_iota(jnp.int32, (C, C), 0)
    ci1 = lax.broadcasted_iota(jnp.int32, (C, C), 1)
    lower1 = ri1 >= ci1
    strict1 = ri1 > ci1
    eye_f = (ri1 == ci1).astype(F32)
    ones_l = jnp.ones((C, LANE), BF16)
    rhs2 = jnp.concatenate([(ri1 <= ci1).astype(BF16), ones_l], axis=1)

    def l2n(t):
        return t * lax.rsqrt(jnp.sum(t * t, axis=-1, keepdims=True) + L2_EPS)

    def rep(m):
        return jnp.concatenate([jnp.broadcast_to(m[c:c + 1, :], (C, C)) for c in range(NC)], axis=0)

    for g in range(HG):
        def conv_silu(x_ref, idx):
            w = cw_ref[g, idx]
            xbuf_ref[0:8, :] = tail_ref[g, idx]
            xbuf_ref[8:8 + TB, :] = x_ref[g]
            tail_ref[g, idx] = x_ref[g, TB - 8:TB, :]
            y = (w[0:1, :] * xbuf_ref[5:5 + TB, :] + w[1:2, :] * xbuf_ref[6:6 + TB, :]
                 + w[2:3, :] * xbuf_ref[7:7 + TB, :] + w[3:4, :] * xbuf_ref[8:8 + TB, :])
            return _silu(y)

        q = l2n(conv_silu(q_ref, 0)) * (HEAD_DIM ** -0.5)
        k = l2n(conv_silu(k_ref, 1))
        v = conv_silu(v_ref, 2)

        beta_b = rep(_sigmoid(b_ref[g]))
        g_b = rep(-jnp.exp(alog_ref[h0 + g]) * _softplus(a_ref[g] + dtb_ref[h0 + g]))
        lhs1 = jnp.concatenate([jnp.where(lower, g_b, 0.0), jnp.where(ri == ci, beta_b, 0.0)], axis=0)
        r1 = _dot_01(lhs1, ones_l)
        gccol = r1[:TB]
        betacol = r1[TB:]
        r2 = _dot_01(g_b, rhs2)
        gcrow = r2[:, :C]
        glast = r2[:, C:]
        dd = jnp.where(lower, gccol[:, :C] - gcrow, 0.0)
        decay_s[g] = jnp.where(lower, jnp.exp(dd), 0.0)
        egc = jnp.exp(gccol)
        kb = k * betacol
        q_s[g] = q
        k_s[g] = k
        kb_s[g] = kb
        rhs_s[g, :, :HEAD_DIM] = v * betacol
        rhs_s[g, :, HEAD_DIM:] = kb * egc
        qg_s[g] = q * egc
        kg_s[g] = k * jnp.exp(glast - gccol)
        egl_s[g] = jnp.exp(glast)

    chains = [(g, c, slice(c * C, (c + 1) * C)) for g in range(HG) for c in range(NC)]
    xs, attns = [], []
    for g, c, rows in chains:
        kc16 = k_s[g, rows, :].astype(BF16)
        lhs = jnp.concatenate([kb_s[g, rows, :], q_s[g, rows, :]], axis=0).astype(BF16)
        pq = _dot_nt(lhs, kc16)
        dec = decay_s[g, rows, :]
        xs.append(-jnp.where(strict1, pq[:C] * dec, 0.0))
        attns.append(jnp.where(lower1, pq[C:] * dec, 0.0).astype(BF16))
    ts = [eye_f + x for x in xs]
    ps = [x.astype(BF16) for x in xs]
    for _ in range(5):
        ps = [_dot(p, p).astype(BF16) for p in ps]
        ts = [t + _dot(t.astype(BF16), p) for t, p in zip(ts, ps)]
    rs = []
    for x, t in zip(xs, ts):
        xh, xl = _split2(x)
        th, tl = _split2(t)
        rs.append((eye_f - t) + _dot_split(xh, xl, th, tl))
    ts = [t + _dot(t.astype(BF16), r.astype(BF16)) for t, r in zip(ts, rs)]
    sols = []
    for (g, c, rows), t in zip(chains, ts):
        th, tl = _split2(t)
        rh, rl = _split2(rhs_s[g, rows, :])
        sols.append(_dot_split(th, tl, rh, rl).astype(BF16))
    for (g, c, rows), attn, sol16 in zip(chains, attns, sols):
        aw = _dot(attn, sol16)
        o0_s[g, rows, :] = aw[:, :HEAD_DIM]
        qt_s[g, rows, :] = (qg_s[g, rows, :] - aw[:, HEAD_DIM:]).astype(BF16)
    for (g, c, rows), sol16 in zip(chains, sols):
        kwu = _dot_tn(kg_s[g, rows, :].astype(BF16), sol16)
        bb_s[g, c] = kwu[:, :HEAD_DIM]
        kw_s[g, c] = kwu[:, HEAD_DIM:].astype(BF16)

    gn = gn_ref[...]
    sts = [state_ref[g] for g in range(HG)]
    for c in range(NC):
        rows = slice(c * C, (c + 1) * C)
        st16 = [st.astype(BF16) for st in sts]
        sts = [sts[g] * egl_s[g, c * C:c * C + 1, :] - _dot(kw_s[g, c], st16[g]) + bb_s[g, c]
               for g in range(HG)]
        for g in range(HG):
            o = _dot(qt_s[g, rows, :], st16[g]) + o0_s[g, rows, :]
            on = o * lax.rsqrt(jnp.mean(o * o, axis=-1, keepdims=True) + NORM_EPS) * gn
            o_ref[rows, g * HEAD_DIM:(g + 1) * HEAD_DIM] = (on * _silu(z_ref[g, rows, :])).astype(o_ref.dtype)
    for g in range(HG):
        state_ref[g] = sts[g]


def _deltanet(dnz, ba, conv_w, a_log, dt_bias, gnorm, tb, hg):
    H = DN_HEADS
    S = dnz.shape[1]
    NC = tb // CHUNK
    nh = H // hg
    grid_spec = pltpu.PrefetchScalarGridSpec(
        num_scalar_prefetch=2,
        grid=(nh, S // tb),
        in_specs=[
            pl.BlockSpec((hg, tb, LANE), lambda h, t, *_: (h, t, 0)),
            pl.BlockSpec((hg, tb, LANE), lambda h, t, *_: (nh + h, t, 0)),
            pl.BlockSpec((hg, tb, LANE), lambda h, t, *_: (2 * nh + h, t, 0)),
            pl.BlockSpec((hg, tb, LANE), lambda h, t, *_: (3 * nh + h, t, 0)),
            pl.BlockSpec((hg, NC, CHUNK), lambda h, t, *_: (h, t, 0)),
            pl.BlockSpec((hg, NC, CHUNK), lambda h, t, *_: (nh + h, t, 0)),
            pl.BlockSpec((hg, 3, DN_CONV, LANE), lambda h, t, *_: (h, 0, 0, 0)),
            pl.BlockSpec((1, LANE), lambda h, t, *_: (0, 0)),
        ],
        out_specs=pl.BlockSpec((tb, hg * LANE), lambda h, t, *_: (t, h)),
        scratch_shapes=[
            pltpu.VMEM((hg, HEAD_DIM, HEAD_DIM), F32),
            pltpu.VMEM((hg, 3, 8, LANE), F32),
            pltpu.VMEM((tb + 8, LANE), F32),
            pltpu.VMEM((hg, tb, LANE), F32),
            pltpu.VMEM((hg, tb, LANE), F32),
            pltpu.VMEM((hg, tb, LANE), F32),
            pltpu.VMEM((hg, tb, 2 * LANE), F32),
            pltpu.VMEM((hg, tb, LANE), F32),
            pltpu.VMEM((hg, tb, LANE), F32),
            pltpu.VMEM((hg, tb, CHUNK), F32),
            pltpu.VMEM((hg, tb, LANE), F32),
            pltpu.VMEM((hg, tb, LANE), BF16),
            pltpu.VMEM((hg, tb, LANE), F32),
            pltpu.VMEM((hg, NC, HEAD_DIM, HEAD_DIM), BF16),
            pltpu.VMEM((hg, NC, HEAD_DIM, HEAD_DIM), F32),
        ],
    )
    return pl.pallas_call(
        _dn_kernel,
        grid_spec=grid_spec,
        out_shape=jax.ShapeDtypeStruct((S, H * HEAD_DIM), BF16),
        compiler_params=_cparams("parallel", "arbitrary"),
        name="deltanet",
    )(a_log, dt_bias, dnz, dnz, dnz, dnz, ba, ba, conv_w, gnorm)


def _sb_kernel(q_ref, k_ref, v_ref, o_ref):
    qi = pl.program_id(1)
    TQ = q_ref.shape[1]
    q = q_ref[0]
    scale = HEAD_DIM ** -0.5
    rq = lax.broadcasted_iota(jnp.int32, (TQ, TQ), 0)
    ck = lax.broadcasted_iota(jnp.int32, (TQ, TQ), 1)
    later = (rq > ck).astype(BF16)

    def cond(carry):
        j, cmax, _, _ = carry
        return jnp.logical_and(j >= 0, cmax > SB_SKIP_LOG)

    def body(carry):
        j, _, c, acc = carry
        k0 = pl.multiple_of(j * TQ, TQ)
        kj = k_ref[0, pl.ds(k0, TQ), :]
        vj = v_ref[0, pl.ds(k0, TQ), :]
        z = _dot_nt(q, kj) * scale
        mask = (ck + j * TQ) < (rq + qi * TQ)
        ls = jnp.minimum(z, 0.0) - jnp.log1p(jnp.exp(-jnp.abs(z)))
        l1m = jnp.where(mask, ls - z, 0.0)
        between = _dot_01(l1m, later)
        a = jnp.where(mask, jnp.exp(ls + between + c), 0.0)
        acc = acc + _dot(a.astype(BF16), vj)
        c = c + jnp.sum(l1m, axis=1, keepdims=True)
        return j - 1, jnp.max(c), c, acc

    init = (qi, jnp.float32(0.0), jnp.zeros((TQ, 1), F32), jnp.zeros((TQ, HEAD_DIM), F32))
    _, _, _, acc = lax.while_loop(cond, body, init)
    o_ref[...] = acc.astype(o_ref.dtype)


def _stick_breaking(sb, tq):
    H = SB_HEADS
    S = sb.shape[1]
    return pl.pallas_call(
        _sb_kernel,
        grid=(H, S // tq),
        in_specs=[
            pl.BlockSpec((1, tq, LANE), lambda h, i: (h, i, 0)),
            pl.BlockSpec((1, S, LANE), lambda h, i: (H + h, 0, 0)),
            pl.BlockSpec((1, S, LANE), lambda h, i: (2 * H + h, 0, 0)),
        ],
        out_specs=pl.BlockSpec((tq, LANE), lambda h, i: (i, h)),
        out_shape=jax.ShapeDtypeStruct((S, H * HEAD_DIM), BF16),
        compiler_params=_cparams("parallel", "arbitrary"),
        name="stick_breaking",
    )(sb, sb, sb)


def _merge_kernel(od_ref, os_ref, gd_ref, gs_ref, wd_ref, ws_ref, o_ref):
    m = (_sigmoid(gd_ref[...]) * _dot(od_ref[...], wd_ref[...])
         + _sigmoid(gs_ref[...]) * _dot(os_ref[...], ws_ref[...]))
    o_ref[...] = m.astype(o_ref.dtype)


def _merge(o_dn, o_sb, gates, w_dn, w_sb, tm, tn):
    S, W = o_dn.shape
    D = w_dn.shape[1]
    nb = D // tn
    return pl.pallas_call(
        _merge_kernel,
        grid=(S // tm, nb),
        in_specs=[
            pl.BlockSpec((tm, W), lambda i, j: (i, 0)),
            pl.BlockSpec((tm, W), lambda i, j: (i, 0)),
            pl.BlockSpec((tm, tn), lambda i, j: (i, j)),
            pl.BlockSpec((tm, tn), lambda i, j: (i, nb + j)),
            pl.BlockSpec((W, tn), lambda i, j: (0, j)),
            pl.BlockSpec((W, tn), lambda i, j: (0, j)),
        ],
        out_specs=pl.BlockSpec((tm, tn), lambda i, j: (i, j)),
        out_shape=jax.ShapeDtypeStruct((S, D), BF16),
        compiler_params=_cparams("parallel", "arbitrary"),
        name="merge",
    )(o_dn, o_sb, gates, gates, w_dn, w_sb)


def _outproj_kernel(m_ref, w_ref, h_ref, g_ref, o_ref):
    o_ref[...] = h_ref[...] + _rms(_dot(m_ref[...], w_ref[...]), g_ref[...])


def _outproj(m, w, h, g, tm):
    S, D = h.shape
    return pl.pallas_call(
        _outproj_kernel,
        grid=(S // tm,),
        in_specs=[
            pl.BlockSpec((tm, D), lambda i: (i, 0)),
            pl.BlockSpec((D, D), lambda i: (0, 0)),
            pl.BlockSpec((tm, D), lambda i: (i, 0)),
            pl.BlockSpec((1, D), lambda i: (0, 0)),
        ],
        out_specs=pl.BlockSpec((tm, D), lambda i: (i, 0)),
        out_shape=jax.ShapeDtypeStruct((S, D), F32),
        compiler_params=_cparams("parallel"),
        name="outproj",
    )(m, w, h, g)


def _ple_kernel(h_ref, p_ref, gpre_ref, wg_ref, wp_ref, gpost_ref, o_ref):
    h = h_ref[...]
    u = _rms(h, gpre_ref[...]).astype(BF16)
    gate = _sigmoid(_dot(u, wg_ref[...]))
    e = _dot(p_ref[...].astype(BF16), wp_ref[...])
    o_ref[...] = h + _rms(gate * e, gpost_ref[...])


def _ple(h, p, gpre, wg, wp, gpost, tm):
    S, D = h.shape
    P = p.shape[1]
    return pl.pallas_call(
        _ple_kernel,
        grid=(S // tm,),
        in_specs=[
            pl.BlockSpec((tm, D), lambda i: (i, 0)),
            pl.BlockSpec((tm, P), lambda i: (i, 0)),
            pl.BlockSpec((1, D), lambda i: (0, 0)),
            pl.BlockSpec((D, D), lambda i: (0, 0)),
            pl.BlockSpec((P, D), lambda i: (0, 0)),
            pl.BlockSpec((1, D), lambda i: (0, 0)),
        ],
        out_specs=pl.BlockSpec((tm, D), lambda i: (i, 0)),
        out_shape=jax.ShapeDtypeStruct((S, D), F32),
        compiler_params=_cparams("parallel"),
        name="ple",
    )(h, p, gpre, wg, wp, gpost)


def _layer(h, p, ffn1_norm_pre, ffn1_w_gate, ffn1_w_up, ffn1_w_down, ffn1_norm_post,
           mix_norm_pre, w_in, dn_conv_w, dn_A_log, dn_dt_bias, dn_out_norm,
           w_branch_dn, w_branch_sb, w_out, mix_norm_post,
           ffn2_norm_pre, ffn2_w_gate, ffn2_w_up, ffn2_w_down, ffn2_norm_post,
           ple_norm_pre, ple_w_gate, ple_w_proj, ple_norm_post):
    S, D = h.shape
    H = DN_HEADS
    dn_w = H * HEAD_DIM
    sb_w = SB_HEADS * HEAD_DIM
    row = lambda g: g.reshape(1, -1).astype(F32)
    bf = lambda w: w.astype(BF16)
    tm = min(512, S)

    h = _ffn(h, row(ffn1_norm_pre), bf(ffn1_w_gate), bf(ffn1_w_up), bf(ffn1_w_down),
             row(ffn1_norm_post), tm, 512)

    o2 = 4 * dn_w
    o4 = o2 + 2 * H
    o5 = o4 + 3 * sb_w
    gpre = row(mix_norm_pre)
    tmp = min(1024, S)
    dnz = _proj_heads(h, gpre, bf(w_in[:, :o2]), F32, tmp, 1024)
    sb = _proj_heads(h, gpre, bf(w_in[:, o4:o5]), BF16, tmp, 1024)
    gates, ba = _proj_rows(h, gpre, bf(w_in[:, o5:]), bf(w_in[:, o2:o4].T), tmp, 1024)

    conv_w = dn_conv_w.reshape(DN_CONV, 3, H, HEAD_DIM).transpose(2, 1, 0, 3)
    o_dn = _deltanet(dnz, ba.reshape(2 * H, S // CHUNK, CHUNK), conv_w,
                     dn_A_log.astype(F32), dn_dt_bias.astype(F32), row(dn_out_norm), min(512, S), 4)
    o_sb = _stick_breaking(sb, min(256, S))

    merged = _merge(o_dn, o_sb, gates, bf(w_branch_dn), bf(w_branch_sb), tm, 1024)
    h = _outproj(merged, bf(w_out), h, row(mix_norm_post), tm)

    h = _ffn(h, row(ffn2_norm_pre), bf(ffn2_w_gate), bf(ffn2_w_up), bf(ffn2_w_down),
             row(ffn2_norm_post), tm, 512)

    return _ple(h, p, row(ple_norm_pre), bf(ple_w_gate), bf(ple_w_proj), row(ple_norm_post), tm)


def kernel(x, p, ffn1_norm_pre, ffn1_w_gate, ffn1_w_up, ffn1_w_down, ffn1_norm_post, mix_norm_pre, w_in, dn_conv_w, dn_A_log, dn_dt_bias, dn_out_norm, w_branch_dn, w_branch_sb, w_out, mix_norm_post, ffn2_norm_pre, ffn2_w_gate, ffn2_w_up, ffn2_w_down, ffn2_norm_post, ple_norm_pre, ple_w_gate, ple_w_proj, ple_norm_post):
    B, S, D = x.shape
    depth = w_in.shape[0]
    outs = []
    for b in range(B):
        h = x[b]
        for i in range(depth):
            h = _layer(h, p[i, b], ffn1_norm_pre[i], ffn1_w_gate[i], ffn1_w_up[i], ffn1_w_down[i],
                       ffn1_norm_post[i], mix_norm_pre[i], w_in[i], dn_conv_w[i], dn_A_log[i],
                       dn_dt_bias[i], dn_out_norm[i], w_branch_dn[i], w_branch_sb[i], w_out[i],
                       mix_norm_post[i], ffn2_norm_pre[i], ffn2_w_gate[i], ffn2_w_up[i],
                       ffn2_w_down[i], ffn2_norm_post[i], ple_norm_pre[i], ple_w_gate[i],
                       ple_w_proj[i], ple_norm_post[i])
        outs.append(h)
    return outs[0].reshape(B, S, D) if B == 1 else jnp.stack(outs)
```

```python
import functools

import jax
import jax.numpy as jnp
from jax import lax
from jax.experimental import pallas as pl
from jax.experimental.pallas import tpu as pltpu

NORM_EPS = 1e-6
L2_EPS = 1e-6
CHUNK = 64
DN_HEADS = 16
SB_HEADS = 16
HEAD_DIM = 128
DN_CONV = 4
LANE = 128
SB_SKIP_LOG = -104.0
VMEM_LIMIT = 56 * 1024 * 1024

F32 = jnp.float32
BF16 = jnp.bfloat16


def _cparams(*sem):
    return pltpu.CompilerParams(dimension_semantics=sem, vmem_limit_bytes=VMEM_LIMIT)


def _rms(x, g):
    return x * lax.rsqrt(jnp.mean(x * x, axis=-1, keepdims=True) + NORM_EPS) * g


def _sigmoid(x):
    return 1.0 / (1.0 + jnp.exp(-x))


def _silu(x):
    return x * _sigmoid(x)


def _softplus(x):
    return jnp.maximum(x, 0.0) + jnp.log(1.0 + jnp.exp(-jnp.abs(x)))


def _dot(a, b):
    return jnp.dot(a, b, preferred_element_type=F32)


def _dot_nt(a, b):
    return lax.dot_general(a, b, (((1,), (1,)), ((), ())), preferred_element_type=F32)


def _dot_tn(a, b):
    return lax.dot_general(a, b, (((0,), (0,)), ((), ())), preferred_element_type=F32)


def _split2(a):
    hi = a.astype(BF16)
    return hi, (a - hi.astype(F32)).astype(BF16)


def _split3(a):
    p1 = a.astype(BF16)
    r = a - p1.astype(F32)
    p2 = r.astype(BF16)
    return p1, p2, (r - p2.astype(F32)).astype(BF16)


def _dot_split(ah, al, bh, bl):
    return _dot(ah, bh) + (_dot(ah, bl) + _dot(al, bh))


def _dot_01(a, b01):
    p1, p2, p3 = _split3(a)
    return _dot(p1, b01) + (_dot(p2, b01) + _dot(p3, b01))


def _ffn_kernel(h_ref, gpre_ref, wg_ref, wu_ref, wd_ref, gpost_ref, o_ref, u_ref, acc_ref):
    j = pl.program_id(1)

    @pl.when(j == 0)
    def _():
        u_ref[...] = _rms(h_ref[...], gpre_ref[...]).astype(BF16)
        acc_ref[...] = jnp.zeros_like(acc_ref)

    u = u_ref[...]
    g = _dot(u, wg_ref[...])
    up = _dot(u, wu_ref[...])
    a = (_silu(g) * up).astype(BF16)
    acc_ref[...] += _dot(a, wd_ref[...])

    @pl.when(j == pl.num_programs(1) - 1)
    def _():
        o_ref[...] = h_ref[...] + 0.5 * _rms(acc_ref[...], gpost_ref[...])


def _ffn(h, gpre, wg, wu, wd, gpost, tm, tf):
    S, D = h.shape
    FF = wg.shape[1]
    return pl.pallas_call(
        _ffn_kernel,
        grid=(S // tm, FF // tf),
        in_specs=[
            pl.BlockSpec((tm, D), lambda i, j: (i, 0)),
            pl.BlockSpec((1, D), lambda i, j: (0, 0)),
            pl.BlockSpec((D, tf), lambda i, j: (0, j)),
            pl.BlockSpec((D, tf), lambda i, j: (0, j)),
            pl.BlockSpec((tf, D), lambda i, j: (j, 0)),
            pl.BlockSpec((1, D), lambda i, j: (0, 0)),
        ],
        out_specs=pl.BlockSpec((tm, D), lambda i, j: (i, 0)),
        out_shape=jax.ShapeDtypeStruct((S, D), F32),
        scratch_shapes=[pltpu.VMEM((tm, D), BF16), pltpu.VMEM((tm, D), F32)],
        compiler_params=_cparams("parallel", "arbitrary"),
        name="ffn",
    )(h, gpre, wg, wu, wd, gpost)


def _proj_heads_kernel(h_ref, g_ref, w_ref, o_ref, u_ref):
    @pl.when(pl.program_id(1) == 0)
    def _():
        u_ref[...] = _rms(h_ref[...], g_ref[...]).astype(BF16)

    res = _dot(u_ref[...], w_ref[...])
    for c in range(o_ref.shape[0]):
        o_ref[c] = res[:, c * LANE:(c + 1) * LANE].astype(o_ref.dtype)


def _proj_heads(h, g, w, col0, N, dtype, tm, tn):
    S, D = h.shape
    j0 = col0 // tn
    return pl.pallas_call(
        _proj_heads_kernel,
        grid=(S // tm, N // tn),
        in_specs=[
            pl.BlockSpec((tm, D), lambda i, j: (i, 0)),
            pl.BlockSpec((1, D), lambda i, j: (0, 0)),
            pl.BlockSpec((D, tn), lambda i, j: (0, j0 + j)),
        ],
        out_specs=pl.BlockSpec((tn // LANE, tm, LANE), lambda i, j: (j, i, 0)),
        out_shape=jax.ShapeDtypeStruct((N // LANE, S, LANE), dtype),
        scratch_shapes=[pltpu.VMEM((tm, D), BF16)],
        compiler_params=_cparams("parallel", "arbitrary"),
        name="proj_heads",
    )(h, g, w)


def _proj_rows_kernel(h_ref, g_ref, w_ref, wt_ref, o_ref, ot_ref, u_ref):
    j = pl.program_id(1)

    @pl.when(j == 0)
    def _():
        u_ref[...] = _rms(h_ref[...], g_ref[...]).astype(BF16)
        ot_ref[...] = _dot_nt(wt_ref[...], u_ref[...])

    o_ref[...] = _dot(u_ref[...], w_ref[...])


def _proj_rows(h, g, w, col0, N, wt, tm, tn):
    S, D = h.shape
    j0 = col0 // tn
    R = wt.shape[0]
    return pl.pallas_call(
        _proj_rows_kernel,
        grid=(S // tm, N // tn),
        in_specs=[
            pl.BlockSpec((tm, D), lambda i, j: (i, 0)),
            pl.BlockSpec((1, D), lambda i, j: (0, 0)),
            pl.BlockSpec((D, tn), lambda i, j: (0, j0 + j)),
            pl.BlockSpec((R, D), lambda i, j: (0, 0)),
        ],
        out_specs=[
            pl.BlockSpec((tm, tn), lambda i, j: (i, j)),
            pl.BlockSpec((R, tm), lambda i, j: (0, i)),
        ],
        out_shape=[jax.ShapeDtypeStruct((S, N), F32), jax.ShapeDtypeStruct((R, S), F32)],
        scratch_shapes=[pltpu.VMEM((tm, D), BF16)],
        compiler_params=_cparams("parallel", "arbitrary"),
        name="proj_rows",
    )(h, g, w, wt)


def _dn_kernel(alog_ref, dtb_ref,
               q_ref, k_ref, v_ref, z_ref, b_ref, a_ref,
               cw_ref, gn_ref, o_ref,
               state_ref, tail_ref, xbuf_ref, q_s, k_s, kb_s, rhs_s, qg_s, kg_s,
               decay_s, egl_s, qt_s, o0_s, kw_s, bb_s):
    HG, TB = q_ref.shape[0], q_ref.shape[1]
    C = CHUNK
    NC = TB // C
    h0 = pl.program_id(0) * HG

    @pl.when(pl.program_id(1) == 0)
    def _():
        state_ref[...] = jnp.zeros_like(state_ref)
        tail_ref[...] = jnp.zeros_like(tail_ref)

    ri = lax.broadcasted_iota(jnp.int32, (TB, C), 0) & (C - 1)
    ci = lax.broadcasted_iota(jnp.int32, (TB, C), 1)
    lower = ri >= ci
    ri1 = lax.broadcasted_iota(jnp.int32, (C, C), 0)
    ci1 = lax.broadcasted_iota(jnp.int32, (C, C), 1)
    lower1 = ri1 >= ci1
    strict1 = ri1 > ci1
    eye_f = (ri1 == ci1).astype(F32)
    ones_l = jnp.ones((C, LANE), BF16)
    upper01 = (ri1 <= ci1).astype(BF16)

    def l2n(t):
        return t * lax.rsqrt(jnp.sum(t * t, axis=-1, keepdims=True) + L2_EPS)

    def rep(m):
        return jnp.concatenate([jnp.broadcast_to(m[c:c + 1, :], (C, C)) for c in range(NC)], axis=0)

    for g in range(HG):
        def conv_silu(x_ref, idx):
            w = cw_ref[g, idx]
            xbuf_ref[0:8, :] = tail_ref[g, idx]
            xbuf_ref[8:8 + TB, :] = x_ref[g]
            tail_ref[g, idx] = x_ref[g, TB - 8:TB, :]
            y = (w[0:1, :] * xbuf_ref[5:5 + TB, :] + w[1:2, :] * xbuf_ref[6:6 + TB, :]
                 + w[2:3, :] * xbuf_ref[7:7 + TB, :] + w[3:4, :] * xbuf_ref[8:8 + TB, :])
            return _silu(y)

        q = l2n(conv_silu(q_ref, 0)) * (HEAD_DIM ** -0.5)
        k = l2n(conv_silu(k_ref, 1))
        v = conv_silu(v_ref, 2)

        beta2 = _sigmoid(b_ref[g])
        g2 = -jnp.exp(alog_ref[h0 + g]) * _softplus(a_ref[g] + dtb_ref[h0 + g])
        gc2 = _dot_01(g2, upper01)
        gcrow = rep(gc2)
        glast = jnp.concatenate(
            [jnp.broadcast_to(gc2[c:c + 1, C - 1:C], (C, LANE)) for c in range(NC)], axis=0)
        gccol = _dot_01(jnp.where(lower, rep(g2), 0.0), ones_l)
        bh, bl = _split2(jnp.where(ri == ci, rep(beta2), 0.0))
        betacol = _dot(bh, ones_l) + _dot(bl, ones_l)
        dd = jnp.where(lower, gccol[:, :C] - gcrow, 0.0)
        decay_s[g] = jnp.where(lower, jnp.exp(dd), 0.0)
        egc = jnp.exp(gccol)
        kb = k * betacol
        q_s[g] = q
        k_s[g] = k
        kb_s[g] = kb
        rhs_s[g, :, :HEAD_DIM] = v * betacol
        rhs_s[g, :, HEAD_DIM:] = kb * egc
        qg_s[g] = q * egc
        kg_s[g] = k * jnp.exp(glast - gccol)
        egl_s[g] = jnp.exp(glast)

    chains = [(g, c, slice(c * C, (c + 1) * C)) for g in range(HG) for c in range(NC)]
    xs, attns = [], []
    for g, c, rows in chains:
        kc16 = k_s[g, rows, :].astype(BF16)
        lhs = jnp.concatenate([kb_s[g, rows, :], q_s[g, rows, :]], axis=0).astype(BF16)
        pq = _dot_nt(lhs, kc16)
        dec = decay_s[g, rows, :]
        xs.append(-jnp.where(strict1, pq[:C] * dec, 0.0))
        attns.append(jnp.where(lower1, pq[C:] * dec, 0.0).astype(BF16))
    ts = [eye_f + x for x in xs]
    ps = [x.astype(BF16) for x in xs]
    for _ in range(5):
        ps = [_dot(p, p).astype(BF16) for p in ps]
        ts = [t + _dot(t.astype(BF16), p) for t, p in zip(ts, ps)]
    rs = []
    for x, t in zip(xs, ts):
        xh, xl = _split2(x)
        th, tl = _split2(t)
        rs.append((eye_f - t) + _dot_split(xh, xl, th, tl))
    ts = [t + _dot(t.astype(BF16), r.astype(BF16)) for t, r in zip(ts, rs)]
    sols = []
    for (g, c, rows), t in zip(chains, ts):
        th, tl = _split2(t)
        rh, rl = _split2(rhs_s[g, rows, :])
        sols.append(_dot_split(th, tl, rh, rl).astype(BF16))
    for (g, c, rows), attn, sol16 in zip(chains, attns, sols):
        aw = _dot(attn, sol16)
        o0_s[g, rows, :] = aw[:, :HEAD_DIM]
        qt_s[g, rows, :] = (qg_s[g, rows, :] - aw[:, HEAD_DIM:]).astype(BF16)
    for (g, c, rows), sol16 in zip(chains, sols):
        kwu = _dot_tn(kg_s[g, rows, :].astype(BF16), sol16)
        bb_s[g, c] = kwu[:, :HEAD_DIM]
        kw_s[g, c] = kwu[:, HEAD_DIM:].astype(BF16)

    gn = gn_ref[...]
    sts = [state_ref[g] for g in range(HG)]
    for c in range(NC):
        rows = slice(c * C, (c + 1) * C)
        st16 = [st.astype(BF16) for st in sts]
        sts = [sts[g] * egl_s[g, c * C:c * C + 1, :] - _dot(kw_s[g, c], st16[g]) + bb_s[g, c]
               for g in range(HG)]
        for g in range(HG):
            o = _dot(qt_s[g, rows, :], st16[g]) + o0_s[g, rows, :]
            on = o * lax.rsqrt(jnp.mean(o * o, axis=-1, keepdims=True) + NORM_EPS) * gn
            o_ref[rows, g * HEAD_DIM:(g + 1) * HEAD_DIM] = (on * _silu(z_ref[g, rows, :])).astype(o_ref.dtype)
    for g in range(HG):
        state_ref[g] = sts[g]


def _deltanet(dnz, ba, conv_w, a_log, dt_bias, gnorm, tb, hg):
    H = DN_HEADS
    S = dnz.shape[1]
    NC = tb // CHUNK
    nh = H // hg
    grid_spec = pltpu.PrefetchScalarGridSpec(
        num_scalar_prefetch=2,
        grid=(nh, S // tb),
        in_specs=[
            pl.BlockSpec((hg, tb, LANE), lambda h, t, *_: (h, t, 0)),
            pl.BlockSpec((hg, tb, LANE), lambda h, t, *_: (nh + h, t, 0)),
            pl.BlockSpec((hg, tb, LANE), lambda h, t, *_: (2 * nh + h, t, 0)),
            pl.BlockSpec((hg, tb, LANE), lambda h, t, *_: (3 * nh + h, t, 0)),
            pl.BlockSpec((hg, NC, CHUNK), lambda h, t, *_: (h, t, 0)),
            pl.BlockSpec((hg, NC, CHUNK), lambda h, t, *_: (nh + h, t, 0)),
            pl.BlockSpec((hg, 3, DN_CONV, LANE), lambda h, t, *_: (h, 0, 0, 0)),
            pl.BlockSpec((1, LANE), lambda h, t, *_: (0, 0)),
        ],
        out_specs=pl.BlockSpec((tb, hg * LANE), lambda h, t, *_: (t, h)),
        scratch_shapes=[
            pltpu.VMEM((hg, HEAD_DIM, HEAD_DIM), F32),
            pltpu.VMEM((hg, 3, 8, LANE), F32),
            pltpu.VMEM((tb + 8, LANE), F32),
            pltpu.VMEM((hg, tb, LANE), F32),
            pltpu.VMEM((hg, tb, LANE), F32),
            pltpu.VMEM((hg, tb, LANE), F32),
            pltpu.VMEM((hg, tb, 2 * LANE), F32),
            pltpu.VMEM((hg, tb, LANE), F32),
            pltpu.VMEM((hg, tb, LANE), F32),
            pltpu.VMEM((hg, tb, CHUNK), F32),
            pltpu.VMEM((hg, tb, LANE), F32),
            pltpu.VMEM((hg, tb, LANE), BF16),
            pltpu.VMEM((hg, tb, LANE), F32),
            pltpu.VMEM((hg, NC, HEAD_DIM, HEAD_DIM), BF16),
            pltpu.VMEM((hg, NC, HEAD_DIM, HEAD_DIM), F32),
        ],
    )
    return pl.pallas_call(
        _dn_kernel,
        grid_spec=grid_spec,
        out_shape=jax.ShapeDtypeStruct((S, H * HEAD_DIM), BF16),
        compiler_params=_cparams("parallel", "arbitrary"),
        name="deltanet",
    )(a_log, dt_bias, dnz, dnz, dnz, dnz, ba, ba, conv_w, gnorm)


SB_T = 128
SB_NEAR = 3


def _sb_kernel(q_ref, k_ref, v_ref, o_ref, acc_s, c_s):
    T = SB_T
    QB = q_ref.shape[1] // T
    qb0 = pl.program_id(1) * QB
    scale = HEAD_DIM ** -0.5
    rq = lax.broadcasted_iota(jnp.int32, (T, T), 0)
    ck = lax.broadcasted_iota(jnp.int32, (T, T), 1)
    diag_mask = ck < rq
    later_ones = jnp.concatenate([(rq > ck).astype(BF16), jnp.ones((T, T), BF16)], axis=1)

    def tiles(qs, kbs, masks, cs):
        kbc = [jnp.maximum(kb, 0) for kb in kbs]
        starts = [pl.multiple_of(kb * T, T) for kb in kbc]
        zs = [_dot_nt(q, k_ref[0, pl.ds(s0, T), :]) * scale for q, s0 in zip(qs, starts)]
        lss, l1ms = [], []
        for z, m in zip(zs, masks):
            ls = jnp.minimum(z, 0.0) - jnp.log(1.0 + jnp.exp(-jnp.abs(z)))
            lss.append(ls)
            l1ms.append(jnp.where(m, ls - z, 0.0))
        sums = []
        for l1m in l1ms:
            hi, lo = _split2(l1m)
            sums.append(_dot(hi, later_ones) + _dot(lo, later_ones))
        outs, new_cs = [], []
        for ls, m, sm, c, s0 in zip(lss, masks, sums, cs, starts):
            a = jnp.where(m, jnp.exp(ls + sm[:, :T] + c), 0.0)
            outs.append(_dot(a.astype(BF16), v_ref[0, pl.ds(s0, T), :]))
            new_cs.append(c + sm[:, T:])
        return outs, new_cs

    qs = [q_ref[0, i * T:(i + 1) * T, :] for i in range(QB)]
    cs = [jnp.zeros((T, T), F32) for _ in range(QB)]
    accs = [jnp.zeros((T, HEAD_DIM), F32) for _ in range(QB)]
    for j in range(SB_NEAR):
        kbs = [qb0 + i - j for i in range(QB)]
        masks = [diag_mask if j == 0 else jnp.broadcast_to(kb >= 0, (T, T)) for kb in kbs]
        outs, cs = tiles(qs, kbs, masks, cs)
        accs = [a + o for a, o in zip(accs, outs)]
    for i in range(QB):
        acc_s[i] = accs[i]
        c_s[i] = cs[i]
    cmax = functools.reduce(jnp.maximum, [jnp.max(c) for c in cs])

    def cond(carry):
        t, cm = carry
        return jnp.logical_and(qb0 + QB - 1 - SB_NEAR - t >= 0, cm > SB_SKIP_LOG)

    def body(carry):
        t, _ = carry
        kbs = [qb0 + i - SB_NEAR - t for i in range(QB)]
        masks = [jnp.broadcast_to(kb >= 0, (T, T)) for kb in kbs]
        outs, new_cs = tiles(qs, kbs, masks, [c_s[i] for i in range(QB)])
        for i in range(QB):
            acc_s[i] += outs[i]
            c_s[i] = new_cs[i]
        return t + 1, functools.reduce(jnp.maximum, [jnp.max(c) for c in new_cs])

    lax.while_loop(cond, body, (jnp.int32(0), cmax))
    for i in range(QB):
        o_ref[i * T:(i + 1) * T, :] = acc_s[i].astype(o_ref.dtype)


def _stick_breaking(sb, tq):
    H = SB_HEADS
    S = sb.shape[1]
    return pl.pallas_call(
        _sb_kernel,
        grid=(H, S // tq),
        in_specs=[
            pl.BlockSpec((1, tq, LANE), lambda h, i: (h, i, 0)),
            pl.BlockSpec((1, S, LANE), lambda h, i: (H + h, 0, 0)),
            pl.BlockSpec((1, S, LANE), lambda h, i: (2 * H + h, 0, 0)),
        ],
        out_specs=pl.BlockSpec((tq, LANE), lambda h, i: (i, h)),
        out_shape=jax.ShapeDtypeStruct((S, H * HEAD_DIM), BF16),
        scratch_shapes=[pltpu.VMEM((tq // SB_T, SB_T, HEAD_DIM), F32),
                        pltpu.VMEM((tq // SB_T, SB_T, SB_T), F32)],
        compiler_params=_cparams("parallel", "arbitrary"),
        name="stick_breaking",
    )(sb, sb, sb)


def _merge_kernel(od_ref, os_ref, gd_ref, gs_ref, wd_ref, ws_ref, o_ref):
    m = (_sigmoid(gd_ref[...]) * _dot(od_ref[...], wd_ref[...])
         + _sigmoid(gs_ref[...]) * _dot(os_ref[...], ws_ref[...]))
    o_ref[...] = m.astype(o_ref.dtype)


def _merge(o_dn, o_sb, gates, w_dn, w_sb, tm, tn):
    S, W = o_dn.shape
    D = w_dn.shape[1]
    nb = D // tn
    return pl.pallas_call(
        _merge_kernel,
        grid=(S // tm, nb),
        in_specs=[
            pl.BlockSpec((tm, W), lambda i, j: (i, 0)),
            pl.BlockSpec((tm, W), lambda i, j: (i, 0)),
            pl.BlockSpec((tm, tn), lambda i, j: (i, j)),
            pl.BlockSpec((tm, tn), lambda i, j: (i, nb + j)),
            pl.BlockSpec((W, tn), lambda i, j: (0, j)),
            pl.BlockSpec((W, tn), lambda i, j: (0, j)),
        ],
        out_specs=pl.BlockSpec((tm, tn), lambda i, j: (i, j)),
        out_shape=jax.ShapeDtypeStruct((S, D), BF16),
        compiler_params=_cparams("parallel", "arbitrary"),
        name="merge",
    )(o_dn, o_sb, gates, gates, w_dn, w_sb)


def _outproj_kernel(m_ref, w_ref, h_ref, g_ref, o_ref):
    o_ref[...] = h_ref[...] + _rms(_dot(m_ref[...], w_ref[...]), g_ref[...])


def _outproj(m, w, h, g, tm):
    S, D = h.shape
    return pl.pallas_call(
        _outproj_kernel,
        grid=(S // tm,),
        in_specs=[
            pl.BlockSpec((tm, D), lambda i: (i, 0)),
            pl.BlockSpec((D, D), lambda i: (0, 0)),
            pl.BlockSpec((tm, D), lambda i: (i, 0)),
            pl.BlockSpec((1, D), lambda i: (0, 0)),
        ],
        out_specs=pl.BlockSpec((tm, D), lambda i: (i, 0)),
        out_shape=jax.ShapeDtypeStruct((S, D), F32),
        compiler_params=_cparams("parallel"),
        name="outproj",
    )(m, w, h, g)


def _ple_kernel(h_ref, p_ref, gpre_ref, wg_ref, wp_ref, gpost_ref, o_ref):
    h = h_ref[...]
    u = _rms(h, gpre_ref[...]).astype(BF16)
    gate = _sigmoid(_dot(u, wg_ref[...]))
    e = _dot(p_ref[...].astype(BF16), wp_ref[...])
    o_ref[...] = h + _rms(gate * e, gpost_ref[...])


def _ple(h, p, gpre, wg, wp, gpost, tm):
    S, D = h.shape
    P = p.shape[1]
    return pl.pallas_call(
        _ple_kernel,
        grid=(S // tm,),
        in_specs=[
            pl.BlockSpec((tm, D), lambda i: (i, 0)),
            pl.BlockSpec((tm, P), lambda i: (i, 0)),
            pl.BlockSpec((1, D), lambda i: (0, 0)),
            pl.BlockSpec((D, D), lambda i: (0, 0)),
            pl.BlockSpec((P, D), lambda i: (0, 0)),
            pl.BlockSpec((1, D), lambda i: (0, 0)),
        ],
        out_specs=pl.BlockSpec((tm, D), lambda i: (i, 0)),
        out_shape=jax.ShapeDtypeStruct((S, D), F32),
        compiler_params=_cparams("parallel"),
        name="ple",
    )(h, p, gpre, wg, wp, gpost)


def _layer(h, p, ffn1_norm_pre, ffn1_w_gate, ffn1_w_up, ffn1_w_down, ffn1_norm_post,
           mix_norm_pre, w_in, dn_conv_w, dn_A_log, dn_dt_bias, dn_out_norm,
           w_branch_dn, w_branch_sb, w_out, mix_norm_post,
           ffn2_norm_pre, ffn2_w_gate, ffn2_w_up, ffn2_w_down, ffn2_norm_post,
           ple_norm_pre, ple_w_gate, ple_w_proj, ple_norm_post):
    S, D = h.shape
    H = DN_HEADS
    dn_w = H * HEAD_DIM
    sb_w = SB_HEADS * HEAD_DIM
    row = lambda g: g.reshape(1, -1).astype(F32)
    bf = lambda w: w.astype(BF16)
    tm = min(512, S)

    h = _ffn(h, row(ffn1_norm_pre), bf(ffn1_w_gate), bf(ffn1_w_up), bf(ffn1_w_down),
             row(ffn1_norm_post), tm, 512)

    o2 = 4 * dn_w
    o4 = o2 + 2 * H
    o5 = o4 + 3 * sb_w
    gpre = row(mix_norm_pre)
    tmp = min(1024, S)
    w_main = jnp.concatenate([w_in[:, :o2], w_in[:, o4:]], axis=1).astype(BF16)
    dnz = _proj_heads(h, gpre, w_main, 0, o2, F32, tmp, 1024)
    sb = _proj_heads(h, gpre, w_main, o2, 3 * sb_w, BF16, tmp, 1024)
    gates, ba = _proj_rows(h, gpre, w_main, o2 + 3 * sb_w, 2 * D, bf(w_in[:, o2:o4].T), tmp, 1024)

    conv_w = dn_conv_w.reshape(DN_CONV, 3, H, HEAD_DIM).transpose(2, 1, 0, 3)
    o_dn = _deltanet(dnz, ba.reshape(2 * H, S // CHUNK, CHUNK), conv_w,
                     dn_A_log.astype(F32), dn_dt_bias.astype(F32), row(dn_out_norm), min(512, S), 4)
    o_sb = _stick_breaking(sb, min(512, S))

    merged = _merge(o_dn, o_sb, gates, bf(w_branch_dn), bf(w_branch_sb), tm, 1024)
    h = _outproj(merged, bf(w_out), h, row(mix_norm_post), tm)

    h = _ffn(h, row(ffn2_norm_pre), bf(ffn2_w_gate), bf(ffn2_w_up), bf(ffn2_w_down),
             row(ffn2_norm_post), tm, 512)

    return _ple(h, p, row(ple_norm_pre), bf(ple_w_gate), bf(ple_w_proj), row(ple_norm_post), tm)


def kernel(x, p, ffn1_norm_pre, ffn1_w_gate, ffn1_w_up, ffn1_w_down, ffn1_norm_post, mix_norm_pre, w_in, dn_conv_w, dn_A_log, dn_dt_bias, dn_out_norm, w_branch_dn, w_branch_sb, w_out, mix_norm_post, ffn2_norm_pre, ffn2_w_gate, ffn2_w_up, ffn2_w_down, ffn2_norm_post, ple_norm_pre, ple_w_gate, ple_w_proj, ple_norm_post):
    B, S, D = x.shape
    depth = w_in.shape[0]
    outs = []
    for b in range(B):
        h = x[b]
        for i in range(depth):
            h = _layer(h, p[i, b], ffn1_norm_pre[i], ffn1_w_gate[i], ffn1_w_up[i], ffn1_w_down[i],
                       ffn1_norm_post[i], mix_norm_pre[i], w_in[i], dn_conv_w[i], dn_A_log[i],
                       dn_dt_bias[i], dn_out_norm[i], w_branch_dn[i], w_branch_sb[i], w_out[i],
                       mix_norm_post[i], ffn2_norm_pre[i], ffn2_w_gate[i], ffn2_w_up[i],
                       ffn2_w_down[i], ffn2_norm_post[i], ple_norm_pre[i], ple_w_gate[i],
                       ple_w_proj[i], ple_norm_post[i])
        outs.append(h)
    return outs[0].reshape(B, S, D) if B == 1 else jnp.stack(outs)
```

```python
import functools

import jax
import jax.numpy as jnp
from jax import lax
from jax.experimental import pallas as pl
from jax.experimental.pallas import tpu as pltpu

NORM_EPS = 1e-6
L2_EPS = 1e-6
CHUNK = 64
DN_HEADS = 16
SB_HEADS = 16
HEAD_DIM = 128
DN_CONV = 4
LANE = 128
SB_SKIP_LOG = -104.0
VMEM_LIMIT = 56 * 1024 * 1024

F32 = jnp.float32
BF16 = jnp.bfloat16


def _cparams(*sem):
    return pltpu.CompilerParams(dimension_semantics=sem, vmem_limit_bytes=VMEM_LIMIT)


def _rms(x, g):
    return x * lax.rsqrt(jnp.mean(x * x, axis=-1, keepdims=True) + NORM_EPS) * g


def _sigmoid(x):
    return 0.5 * jnp.tanh(0.5 * x) + 0.5


def _silu(x):
    return x * _sigmoid(x)


def _softplus(x):
    return jnp.maximum(x, 0.0) + jnp.log(1.0 + jnp.exp(-jnp.abs(x)))


def _dot(a, b):
    return jnp.dot(a, b, preferred_element_type=F32)


def _dot_nt(a, b):
    return lax.dot_general(a, b, (((1,), (1,)), ((), ())), preferred_element_type=F32)


def _dot_tn(a, b):
    return lax.dot_general(a, b, (((0,), (0,)), ((), ())), preferred_element_type=F32)


def _split2(a):
    hi = a.astype(BF16)
    return hi, (a - hi.astype(F32)).astype(BF16)


def _split3(a):
    p1 = a.astype(BF16)
    r = a - p1.astype(F32)
    p2 = r.astype(BF16)
    return p1, p2, (r - p2.astype(F32)).astype(BF16)


def _dot_split(ah, al, bh, bl):
    return _dot(jnp.concatenate([ah, ah, al], axis=1), jnp.concatenate([bh, bl, bh], axis=0))


def _dot_01(a, b01):
    return _dot(jnp.concatenate(_split3(a), axis=1), jnp.concatenate([b01, b01, b01], axis=0))


def _ffn_kernel(h_ref, gpre_ref, wg_ref, wu_ref, wd_ref, gpost_ref, o_ref, u_ref, acc_ref):
    j = pl.program_id(1)

    @pl.when(j == 0)
    def _():
        u_ref[...] = _rms(h_ref[...], gpre_ref[...]).astype(BF16)
        acc_ref[...] = jnp.zeros_like(acc_ref)

    u = u_ref[...]
    g = _dot(u, wg_ref[...])
    up = _dot(u, wu_ref[...])
    a = (_silu(g) * up).astype(BF16)
    acc_ref[...] += _dot(a, wd_ref[...])

    @pl.when(j == pl.num_programs(1) - 1)
    def _():
        o_ref[...] = h_ref[...] + 0.5 * _rms(acc_ref[...], gpost_ref[...])


def _ffn(h, gpre, wg, wu, wd, gpost, tm, tf):
    S, D = h.shape
    FF = wg.shape[1]
    return pl.pallas_call(
        _ffn_kernel,
        grid=(S // tm, FF // tf),
        in_specs=[
            pl.BlockSpec((tm, D), lambda i, j: (i, 0)),
            pl.BlockSpec((1, D), lambda i, j: (0, 0)),
            pl.BlockSpec((D, tf), lambda i, j: (0, j)),
            pl.BlockSpec((D, tf), lambda i, j: (0, j)),
            pl.BlockSpec((tf, D), lambda i, j: (j, 0)),
            pl.BlockSpec((1, D), lambda i, j: (0, 0)),
        ],
        out_specs=pl.BlockSpec((tm, D), lambda i, j: (i, 0)),
        out_shape=jax.ShapeDtypeStruct((S, D), F32),
        scratch_shapes=[pltpu.VMEM((tm, D), BF16), pltpu.VMEM((tm, D), F32)],
        compiler_params=_cparams("parallel", "arbitrary"),
        name="ffn",
    )(h, gpre, wg, wu, wd, gpost)


def _proj_heads_kernel(h_ref, g_ref, w_ref, o_ref, u_ref):
    @pl.when(pl.program_id(1) == 0)
    def _():
        u_ref[...] = _rms(h_ref[...], g_ref[...]).astype(BF16)

    res = _dot(u_ref[...], w_ref[...])
    for c in range(o_ref.shape[0]):
        o_ref[c] = res[:, c * LANE:(c + 1) * LANE].astype(o_ref.dtype)


def _proj_heads(h, g, w, col0, N, dtype, tm, tn):
    S, D = h.shape
    j0 = col0 // tn
    return pl.pallas_call(
        _proj_heads_kernel,
        grid=(S // tm, N // tn),
        in_specs=[
            pl.BlockSpec((tm, D), lambda i, j: (i, 0)),
            pl.BlockSpec((1, D), lambda i, j: (0, 0)),
            pl.BlockSpec((D, tn), lambda i, j: (0, j0 + j)),
        ],
        out_specs=pl.BlockSpec((tn // LANE, tm, LANE), lambda i, j: (j, i, 0)),
        out_shape=jax.ShapeDtypeStruct((N // LANE, S, LANE), dtype),
        scratch_shapes=[pltpu.VMEM((tm, D), BF16)],
        compiler_params=_cparams("parallel", "arbitrary"),
        name="proj_heads",
    )(h, g, w)


def _proj_rows_kernel(h_ref, g_ref, w_ref, wt_ref, o_ref, ot_ref, u_ref):
    j = pl.program_id(1)

    @pl.when(j == 0)
    def _():
        u_ref[...] = _rms(h_ref[...], g_ref[...]).astype(BF16)
        ot_ref[...] = _dot_nt(wt_ref[...], u_ref[...])

    o_ref[...] = _dot(u_ref[...], w_ref[...])


def _proj_rows(h, g, w, col0, N, wt, tm, tn):
    S, D = h.shape
    j0 = col0 // tn
    R = wt.shape[0]
    return pl.pallas_call(
        _proj_rows_kernel,
        grid=(S // tm, N // tn),
        in_specs=[
            pl.BlockSpec((tm, D), lambda i, j: (i, 0)),
            pl.BlockSpec((1, D), lambda i, j: (0, 0)),
            pl.BlockSpec((D, tn), lambda i, j: (0, j0 + j)),
            pl.BlockSpec((R, D), lambda i, j: (0, 0)),
        ],
        out_specs=[
            pl.BlockSpec((tm, tn), lambda i, j: (i, j)),
            pl.BlockSpec((R, tm), lambda i, j: (0, i)),
        ],
        out_shape=[jax.ShapeDtypeStruct((S, N), F32), jax.ShapeDtypeStruct((R, S), F32)],
        scratch_shapes=[pltpu.VMEM((tm, D), BF16)],
        compiler_params=_cparams("parallel", "arbitrary"),
        name="proj_rows",
    )(h, g, w, wt)


def _dn_kernel(alog_ref, dtb_ref,
               q_ref, k_ref, v_ref, z_ref, b_ref, a_ref,
               cw_ref, gn_ref, o_ref,
               state_ref, tail_ref, xbuf_ref, q_s, k_s, kb_s, rhs_s, qg_s, kg_s,
               decay_s, egl_s, qt_s, o0_s, kw_s, bb_s):
    HG, TB = q_ref.shape[0], q_ref.shape[1]
    C = CHUNK
    NC = TB // C
    h0 = pl.program_id(0) * HG

    @pl.when(pl.program_id(1) == 0)
    def _():
        state_ref[...] = jnp.zeros_like(state_ref)
        tail_ref[...] = jnp.zeros_like(tail_ref)

    ri = lax.broadcasted_iota(jnp.int32, (TB, C), 0) & (C - 1)
    ci = lax.broadcasted_iota(jnp.int32, (TB, C), 1)
    lower = ri >= ci
    ri1 = lax.broadcasted_iota(jnp.int32, (C, C), 0)
    ci1 = lax.broadcasted_iota(jnp.int32, (C, C), 1)
    lower1 = ri1 >= ci1
    strict1 = ri1 > ci1
    eye_f = (ri1 == ci1).astype(F32)
    ones_l = jnp.ones((C, LANE), BF16)
    ones_2 = jnp.ones((2 * C, LANE), BF16)
    upper01 = (ri1 <= ci1).astype(BF16)

    def l2n(t):
        return t * lax.rsqrt(jnp.sum(t * t, axis=-1, keepdims=True) + L2_EPS)

    def rep(m):
        return jnp.concatenate([jnp.broadcast_to(m[c:c + 1, :], (C, C)) for c in range(NC)], axis=0)

    for g in range(HG):
        def conv_silu(x_ref, idx):
            w = cw_ref[g, idx]
            xbuf_ref[0:8, :] = tail_ref[g, idx]
            xbuf_ref[8:8 + TB, :] = x_ref[g]
            tail_ref[g, idx] = x_ref[g, TB - 8:TB, :]
            y = (w[0:1, :] * xbuf_ref[5:5 + TB, :] + w[1:2, :] * xbuf_ref[6:6 + TB, :]
                 + w[2:3, :] * xbuf_ref[7:7 + TB, :] + w[3:4, :] * xbuf_ref[8:8 + TB, :])
            return _silu(y)

        q = l2n(conv_silu(q_ref, 0)) * (HEAD_DIM ** -0.5)
        k = l2n(conv_silu(k_ref, 1))
        v = conv_silu(v_ref, 2)

        beta2 = _sigmoid(b_ref[g])
        g2 = -jnp.exp(alog_ref[h0 + g]) * _softplus(a_ref[g] + dtb_ref[h0 + g])
        gc2 = _dot_01(g2, upper01)
        gcrow = rep(gc2)
        glast = jnp.concatenate(
            [jnp.broadcast_to(gc2[c:c + 1, C - 1:C], (C, LANE)) for c in range(NC)], axis=0)
        gccol = _dot_01(jnp.where(lower, rep(g2), 0.0), ones_l)
        betacol = _dot(jnp.concatenate(_split2(jnp.where(ri == ci, rep(beta2), 0.0)), axis=1), ones_2)
        dd = jnp.where(lower, gccol[:, :C] - gcrow, 0.0)
        decay_s[g] = jnp.where(lower, jnp.exp(dd), 0.0)
        egc = jnp.exp(gccol)
        kb = k * betacol
        q_s[g] = q
        k_s[g] = k
        kb_s[g] = kb
        rhs_s[g, :, :HEAD_DIM] = v * betacol
        rhs_s[g, :, HEAD_DIM:] = kb * egc
        qg_s[g] = q * egc
        kg_s[g] = k * jnp.exp(glast - gccol)
        egl_s[g] = jnp.exp(glast)

    chains = [(g, c, slice(c * C, (c + 1) * C)) for g in range(HG) for c in range(NC)]
    xs, attns = [], []
    for g, c, rows in chains:
        kc16 = k_s[g, rows, :].astype(BF16)
        lhs = jnp.concatenate([kb_s[g, rows, :], q_s[g, rows, :]], axis=0).astype(BF16)
        pq = _dot_nt(lhs, kc16)
        dec = decay_s[g, rows, :]
        xs.append(-jnp.where(strict1, pq[:C] * dec, 0.0))
        attns.append(jnp.where(lower1, pq[C:] * dec, 0.0).astype(BF16))
    ts = [eye_f + x for x in xs]
    ps = [x.astype(BF16) for x in xs]
    for _ in range(5):
        ps = [_dot(p, p).astype(BF16) for p in ps]
        ts = [t + _dot(t.astype(BF16), p) for t, p in zip(ts, ps)]
    rs = []
    for x, t in zip(xs, ts):
        xh, xl = _split2(x)
        th, tl = _split2(t)
        rs.append((eye_f - t) + _dot_split(xh, xl, th, tl))
    ts = [t + _dot(t.astype(BF16), r.astype(BF16)) for t, r in zip(ts, rs)]
    sols = []
    for (g, c, rows), t in zip(chains, ts):
        th, tl = _split2(t)
        rh, rl = _split2(rhs_s[g, rows, :])
        sols.append(_dot_split(th, tl, rh, rl).astype(BF16))
    for (g, c, rows), attn, sol16 in zip(chains, attns, sols):
        aw = _dot(attn, sol16)
        o0_s[g, rows, :] = aw[:, :HEAD_DIM]
        qt_s[g, rows, :] = (qg_s[g, rows, :] - aw[:, HEAD_DIM:]).astype(BF16)
    for (g, c, rows), sol16 in zip(chains, sols):
        kwu = _dot_tn(kg_s[g, rows, :].astype(BF16), sol16)
        bb_s[g, c] = kwu[:, :HEAD_DIM]
        kw_s[g, c] = kwu[:, HEAD_DIM:].astype(BF16)

    gn = gn_ref[...]
    sts = [state_ref[g] for g in range(HG)]
    for c in range(NC):
        rows = slice(c * C, (c + 1) * C)
        st16 = [st.astype(BF16) for st in sts]
        sts = [sts[g] * egl_s[g, c * C:c * C + 1, :] - _dot(kw_s[g, c], st16[g]) + bb_s[g, c]
               for g in range(HG)]
        for g in range(HG):
            o = _dot(qt_s[g, rows, :], st16[g]) + o0_s[g, rows, :]
            on = o * lax.rsqrt(jnp.mean(o * o, axis=-1, keepdims=True) + NORM_EPS) * gn
            o_ref[rows, g * HEAD_DIM:(g + 1) * HEAD_DIM] = (on * _silu(z_ref[g, rows, :])).astype(o_ref.dtype)
    for g in range(HG):
        state_ref[g] = sts[g]


def _deltanet(dnz, ba, conv_w, a_log, dt_bias, gnorm, tb, hg):
    H = DN_HEADS
    S = dnz.shape[1]
    NC = tb // CHUNK
    nh = H // hg
    grid_spec = pltpu.PrefetchScalarGridSpec(
        num_scalar_prefetch=2,
        grid=(nh, S // tb),
        in_specs=[
            pl.BlockSpec((hg, tb, LANE), lambda h, t, *_: (h, t, 0)),
            pl.BlockSpec((hg, tb, LANE), lambda h, t, *_: (nh + h, t, 0)),
            pl.BlockSpec((hg, tb, LANE), lambda h, t, *_: (2 * nh + h, t, 0)),
            pl.BlockSpec((hg, tb, LANE), lambda h, t, *_: (3 * nh + h, t, 0)),
            pl.BlockSpec((hg, NC, CHUNK), lambda h, t, *_: (h, t, 0)),
            pl.BlockSpec((hg, NC, CHUNK), lambda h, t, *_: (nh + h, t, 0)),
            pl.BlockSpec((hg, 3, DN_CONV, LANE), lambda h, t, *_: (h, 0, 0, 0)),
            pl.BlockSpec((1, LANE), lambda h, t, *_: (0, 0)),
        ],
        out_specs=pl.BlockSpec((tb, hg * LANE), lambda h, t, *_: (t, h)),
        scratch_shapes=[
            pltpu.VMEM((hg, HEAD_DIM, HEAD_DIM), F32),
            pltpu.VMEM((hg, 3, 8, LANE), F32),
            pltpu.VMEM((tb + 8, LANE), F32),
            pltpu.VMEM((hg, tb, LANE), F32),
            pltpu.VMEM((hg, tb, LANE), F32),
            pltpu.VMEM((hg, tb, LANE), F32),
            pltpu.VMEM((hg, tb, 2 * LANE), F32),
            pltpu.VMEM((hg, tb, LANE), F32),
            pltpu.VMEM((hg, tb, LANE), F32),
            pltpu.VMEM((hg, tb, CHUNK), F32),
            pltpu.VMEM((hg, tb, LANE), F32),
            pltpu.VMEM((hg, tb, LANE), BF16),
            pltpu.VMEM((hg, tb, LANE), F32),
            pltpu.VMEM((hg, NC, HEAD_DIM, HEAD_DIM), BF16),
            pltpu.VMEM((hg, NC, HEAD_DIM, HEAD_DIM), F32),
        ],
    )
    return pl.pallas_call(
        _dn_kernel,
        grid_spec=grid_spec,
        out_shape=jax.ShapeDtypeStruct((S, H * HEAD_DIM), BF16),
        compiler_params=_cparams("parallel", "arbitrary"),
        name="deltanet",
    )(a_log, dt_bias, dnz, dnz, dnz, dnz, ba, ba, conv_w, gnorm)


SB_T = 128
SB_NEAR = 3


def _sb_kernel(q_ref, k_ref, v_ref, o_ref, acc_s, c_s):
    T = SB_T
    QB = q_ref.shape[1] // T
    qb0 = pl.program_id(1) * QB
    scale = HEAD_DIM ** -0.5
    rq = lax.broadcasted_iota(jnp.int32, (T, T), 0)
    ck = lax.broadcasted_iota(jnp.int32, (T, T), 1)
    diag_mask = ck < rq
    later_ones = jnp.concatenate([(rq > ck).astype(BF16), jnp.ones((T, T), BF16)], axis=1)
    later_ones2 = jnp.concatenate([later_ones, later_ones], axis=0)

    def tiles(qs, kbs, masks, cs):
        kbc = [jnp.maximum(kb, 0) for kb in kbs]
        starts = [pl.multiple_of(kb * T, T) for kb in kbc]
        zs = [_dot_nt(q, k_ref[0, pl.ds(s0, T), :]) * scale for q, s0 in zip(qs, starts)]
        lss, l1ms = [], []
        for z, m in zip(zs, masks):
            ls = jnp.minimum(z, 0.0) - jnp.log(1.0 + jnp.exp(-jnp.abs(z)))
            lss.append(ls)
            l1ms.append(jnp.where(m, ls - z, 0.0))
        sums = []
        for l1m in l1ms:
            sums.append(_dot(jnp.concatenate(_split2(l1m), axis=1), later_ones2))
        outs, new_cs = [], []
        for ls, m, sm, c, s0 in zip(lss, masks, sums, cs, starts):
            a = jnp.where(m, jnp.exp(ls + sm[:, :T] + c), 0.0)
            outs.append(_dot(a.astype(BF16), v_ref[0, pl.ds(s0, T), :]))
            new_cs.append(c + sm[:, T:])
        return outs, new_cs

    qs = [q_ref[0, i * T:(i + 1) * T, :] for i in range(QB)]
    cs = [jnp.zeros((T, T), F32) for _ in range(QB)]
    accs = [jnp.zeros((T, HEAD_DIM), F32) for _ in range(QB)]
    for j in range(SB_NEAR):
        kbs = [qb0 + i - j for i in range(QB)]
        masks = [diag_mask if j == 0 else jnp.broadcast_to(kb >= 0, (T, T)) for kb in kbs]
        outs, cs = tiles(qs, kbs, masks, cs)
        accs = [a + o for a, o in zip(accs, outs)]
    for i in range(QB):
        acc_s[i] = accs[i]
        c_s[i] = cs[i]
    cmax = functools.reduce(jnp.maximum, [jnp.max(c) for c in cs])

    def cond(carry):
        t, cm = carry
        return jnp.logical_and(qb0 + QB - 1 - SB_NEAR - t >= 0, cm > SB_SKIP_LOG)

    def body(carry):
        t, _ = carry
        kbs = [qb0 + i - SB_NEAR - t for i in range(QB)]
        masks = [jnp.broadcast_to(kb >= 0, (T, T)) for kb in kbs]
        outs, new_cs = tiles(qs, kbs, masks, [c_s[i] for i in range(QB)])
        for i in range(QB):
            acc_s[i] += outs[i]
            c_s[i] = new_cs[i]
        return t + 1, functools.reduce(jnp.maximum, [jnp.max(c) for c in new_cs])

    lax.while_loop(cond, body, (jnp.int32(0), cmax))
    for i in range(QB):
        o_ref[i * T:(i + 1) * T, :] = acc_s[i].astype(o_ref.dtype)


def _stick_breaking(sb, tq):
    H = SB_HEADS
    S = sb.shape[1]
    return pl.pallas_call(
        _sb_kernel,
        grid=(H, S // tq),
        in_specs=[
            pl.BlockSpec((1, tq, LANE), lambda h, i: (h, i, 0)),
            pl.BlockSpec((1, S, LANE), lambda h, i: (H + h, 0, 0)),
            pl.BlockSpec((1, S, LANE), lambda h, i: (2 * H + h, 0, 0)),
        ],
        out_specs=pl.BlockSpec((tq, LANE), lambda h, i: (i, h)),
        out_shape=jax.ShapeDtypeStruct((S, H * HEAD_DIM), BF16),
        scratch_shapes=[pltpu.VMEM((tq // SB_T, SB_T, HEAD_DIM), F32),
                        pltpu.VMEM((tq // SB_T, SB_T, SB_T), F32)],
        compiler_params=_cparams("parallel", "arbitrary"),
        name="stick_breaking",
    )(sb, sb, sb)


def _merge_kernel(od_ref, os_ref, gd_ref, gs_ref, wd_ref, ws_ref, o_ref):
    m = (_sigmoid(gd_ref[...]) * _dot(od_ref[...], wd_ref[...])
         + _sigmoid(gs_ref[...]) * _dot(os_ref[...], ws_ref[...]))
    o_ref[...] = m.astype(o_ref.dtype)


def _merge(o_dn, o_sb, gates, w_dn, w_sb, tm, tn):
    S, W = o_dn.shape
    D = w_dn.shape[1]
    nb = D // tn
    return pl.pallas_call(
        _merge_kernel,
        grid=(S // tm, nb),
        in_specs=[
            pl.BlockSpec((tm, W), lambda i, j: (i, 0)),
            pl.BlockSpec((tm, W), lambda i, j: (i, 0)),
            pl.BlockSpec((tm, tn), lambda i, j: (i, j)),
            pl.BlockSpec((tm, tn), lambda i, j: (i, nb + j)),
            pl.BlockSpec((W, tn), lambda i, j: (0, j)),
            pl.BlockSpec((W, tn), lambda i, j: (0, j)),
        ],
        out_specs=pl.BlockSpec((tm, tn), lambda i, j: (i, j)),
        out_shape=jax.ShapeDtypeStruct((S, D), BF16),
        compiler_params=_cparams("parallel", "arbitrary"),
        name="merge",
    )(o_dn, o_sb, gates, gates, w_dn, w_sb)


def _outproj_kernel(m_ref, w_ref, h_ref, g_ref, o_ref):
    o_ref[...] = h_ref[...] + _rms(_dot(m_ref[...], w_ref[...]), g_ref[...])


def _outproj(m, w, h, g, tm):
    S, D = h.shape
    return pl.pallas_call(
        _outproj_kernel,
        grid=(S // tm,),
        in_specs=[
            pl.BlockSpec((tm, D), lambda i: (i, 0)),
            pl.BlockSpec((D, D), lambda i: (0, 0)),
            pl.BlockSpec((tm, D), lambda i: (i, 0)),
            pl.BlockSpec((1, D), lambda i: (0, 0)),
        ],
        out_specs=pl.BlockSpec((tm, D), lambda i: (i, 0)),
        out_shape=jax.ShapeDtypeStruct((S, D), F32),
        compiler_params=_cparams("parallel"),
        name="outproj",
    )(m, w, h, g)


def _ple_kernel(h_ref, p_ref, gpre_ref, wg_ref, wp_ref, gpost_ref, o_ref):
    h = h_ref[...]
    u = _rms(h, gpre_ref[...]).astype(BF16)
    gate = _sigmoid(_dot(u, wg_ref[...]))
    e = _dot(p_ref[...].astype(BF16), wp_ref[...])
    o_ref[...] = h + _rms(gate * e, gpost_ref[...])


def _ple(h, p, gpre, wg, wp, gpost, tm):
    S, D = h.shape
    P = p.shape[1]
    return pl.pallas_call(
        _ple_kernel,
        grid=(S // tm,),
        in_specs=[
            pl.BlockSpec((tm, D), lambda i: (i, 0)),
            pl.BlockSpec((tm, P), lambda i: (i, 0)),
            pl.BlockSpec((1, D), lambda i: (0, 0)),
            pl.BlockSpec((D, D), lambda i: (0, 0)),
            pl.BlockSpec((P, D), lambda i: (0, 0)),
            pl.BlockSpec((1, D), lambda i: (0, 0)),
        ],
        out_specs=pl.BlockSpec((tm, D), lambda i: (i, 0)),
        out_shape=jax.ShapeDtypeStruct((S, D), F32),
        compiler_params=_cparams("parallel"),
        name="ple",
    )(h, p, gpre, wg, wp, gpost)


def _layer(h, p, ffn1_norm_pre, ffn1_w_gate, ffn1_w_up, ffn1_w_down, ffn1_norm_post,
           mix_norm_pre, w_in, dn_conv_w, dn_A_log, dn_dt_bias, dn_out_norm,
           w_branch_dn, w_branch_sb, w_out, mix_norm_post,
           ffn2_norm_pre, ffn2_w_gate, ffn2_w_up, ffn2_w_down, ffn2_norm_post,
           ple_norm_pre, ple_w_gate, ple_w_proj, ple_norm_post):
    S, D = h.shape
    H = DN_HEADS
    dn_w = H * HEAD_DIM
    sb_w = SB_HEADS * HEAD_DIM
    row = lambda g: g.reshape(1, -1).astype(F32)
    bf = lambda w: w.astype(BF16)
    tm = min(512, S)

    h = _ffn(h, row(ffn1_norm_pre), bf(ffn1_w_gate), bf(ffn1_w_up), bf(ffn1_w_down),
             row(ffn1_norm_post), tm, 512)

    o2 = 4 * dn_w
    o4 = o2 + 2 * H
    o5 = o4 + 3 * sb_w
    gpre = row(mix_norm_pre)
    tmp = min(1024, S)
    w_bf = bf(w_in)
    w_b = w_bf[:, o4:]
    dnz = _proj_heads(h, gpre, w_bf, 0, o2, F32, tmp, 1024)
    sb = _proj_heads(h, gpre, w_b, 0, 3 * sb_w, BF16, tmp, 1024)
    gates, ba = _proj_rows(h, gpre, w_b, 3 * sb_w, 2 * D, bf(w_in[:, o2:o4].T), tmp, 1024)

    conv_w = dn_conv_w.reshape(DN_CONV, 3, H, HEAD_DIM).transpose(2, 1, 0, 3)
    o_dn = _deltanet(dnz, ba.reshape(2 * H, S // CHUNK, CHUNK), conv_w,
                     dn_A_log.astype(F32), dn_dt_bias.astype(F32), row(dn_out_norm), min(512, S), 4)
    o_sb = _stick_breaking(sb, min(2048, S))

    merged = _merge(o_dn, o_sb, gates, bf(w_branch_dn), bf(w_branch_sb), tm, 1024)
    h = _outproj(merged, bf(w_out), h, row(mix_norm_post), tm)

    h = _ffn(h, row(ffn2_norm_pre), bf(ffn2_w_gate), bf(ffn2_w_up), bf(ffn2_w_down),
             row(ffn2_norm_post), tm, 512)

    return _ple(h, p, row(ple_norm_pre), bf(ple_w_gate), bf(ple_w_proj), row(ple_norm_post), tm)


def kernel(x, p, ffn1_norm_pre, ffn1_w_gate, ffn1_w_up, ffn1_w_down, ffn1_norm_post, mix_norm_pre, w_in, dn_conv_w, dn_A_log, dn_dt_bias, dn_out_norm, w_branch_dn, w_branch_sb, w_out, mix_norm_post, ffn2_norm_pre, ffn2_w_gate, ffn2_w_up, ffn2_w_down, ffn2_norm_post, ple_norm_pre, ple_w_gate, ple_w_proj, ple_norm_post):
    B, S, D = x.shape
    depth = w_in.shape[0]
    outs = []
    for b in range(B):
        h = x[b]
        for i in range(depth):
            h = _layer(h, p[i, b], ffn1_norm_pre[i], ffn1_w_gate[i], ffn1_w_up[i], ffn1_w_down[i],
                       ffn1_norm_post[i], mix_norm_pre[i], w_in[i], dn_conv_w[i], dn_A_log[i],
                       dn_dt_bias[i], dn_out_norm[i], w_branch_dn[i], w_branch_sb[i], w_out[i],
                       mix_norm_post[i], ffn2_norm_pre[i], ffn2_w_gate[i], ffn2_w_up[i],
                       ffn2_w_down[i], ffn2_norm_post[i], ple_norm_pre[i], ple_w_gate[i],
                       ple_w_proj[i], ple_norm_post[i])
        outs.append(h)
    return outs[0].reshape(B, S, D) if B == 1 else jnp.stack(outs)
```

```python
import functools

import jax
import jax.numpy as jnp
from jax import lax
from jax.experimental import pallas as pl
from jax.experimental.pallas import tpu as pltpu

NORM_EPS = 1e-6
L2_EPS = 1e-6
CHUNK = 64
DN_HEADS = 16
SB_HEADS = 16
HEAD_DIM = 128
DN_CONV = 4
LANE = 128
SB_SKIP_LOG = -104.0
VMEM_LIMIT = 56 * 1024 * 1024

F32 = jnp.float32
BF16 = jnp.bfloat16


def _cparams(*sem):
    return pltpu.CompilerParams(dimension_semantics=sem, vmem_limit_bytes=VMEM_LIMIT)


def _rms(x, g):
    return x * lax.rsqrt(jnp.mean(x * x, axis=-1, keepdims=True) + NORM_EPS) * g


def _sigmoid(x):
    return 0.5 * jnp.tanh(0.5 * x) + 0.5


def _silu(x):
    h = 0.5 * x
    return h + h * jnp.tanh(h)


def _softplus(x):
    return jnp.maximum(x, 0.0) + jnp.log(1.0 + jnp.exp(-jnp.abs(x)))


def _dot(a, b):
    return jnp.dot(a, b, preferred_element_type=F32)


def _dot_nt(a, b):
    return lax.dot_general(a, b, (((1,), (1,)), ((), ())), preferred_element_type=F32)


def _dot_tn(a, b):
    return lax.dot_general(a, b, (((0,), (0,)), ((), ())), preferred_element_type=F32)


def _split2(a):
    hi = a.astype(BF16)
    return hi, (a - hi.astype(F32)).astype(BF16)


def _split3(a):
    p1 = a.astype(BF16)
    r = a - p1.astype(F32)
    p2 = r.astype(BF16)
    return p1, p2, (r - p2.astype(F32)).astype(BF16)


def _dot_split(ah, al, bh, bl):
    return _dot(jnp.concatenate([ah, ah, al], axis=1), jnp.concatenate([bh, bl, bh], axis=0))


def _dot_01(a, b01):
    return _dot(jnp.concatenate(_split3(a), axis=1), jnp.concatenate([b01, b01, b01], axis=0))


def _ffn_kernel(h_ref, gpre_ref, wg_ref, wu_ref, wd_ref, gpost_ref, o_ref, u_ref, acc_ref):
    j = pl.program_id(1)

    @pl.when(j == 0)
    def _():
        u_ref[...] = _rms(h_ref[...], gpre_ref[...]).astype(BF16)
        acc_ref[...] = jnp.zeros_like(acc_ref)

    u = u_ref[...]
    g = _dot(u, wg_ref[...])
    up = _dot(u, wu_ref[...])
    a = (_silu(g) * up).astype(BF16)
    acc_ref[...] += _dot(a, wd_ref[...])

    @pl.when(j == pl.num_programs(1) - 1)
    def _():
        o_ref[...] = h_ref[...] + 0.5 * _rms(acc_ref[...], gpost_ref[...])


def _ffn(h, gpre, wg, wu, wd, gpost, tm, tf):
    S, D = h.shape
    FF = wg.shape[1]
    return pl.pallas_call(
        _ffn_kernel,
        grid=(S // tm, FF // tf),
        in_specs=[
            pl.BlockSpec((tm, D), lambda i, j: (i, 0)),
            pl.BlockSpec((1, D), lambda i, j: (0, 0)),
            pl.BlockSpec((D, tf), lambda i, j: (0, j)),
            pl.BlockSpec((D, tf), lambda i, j: (0, j)),
            pl.BlockSpec((tf, D), lambda i, j: (j, 0)),
            pl.BlockSpec((1, D), lambda i, j: (0, 0)),
        ],
        out_specs=pl.BlockSpec((tm, D), lambda i, j: (i, 0)),
        out_shape=jax.ShapeDtypeStruct((S, D), F32),
        scratch_shapes=[pltpu.VMEM((tm, D), BF16), pltpu.VMEM((tm, D), F32)],
        compiler_params=_cparams("parallel", "arbitrary"),
        name="ffn",
    )(h, gpre, wg, wu, wd, gpost)


def _proj_heads_kernel(h_ref, g_ref, w_ref, o_ref, u_ref):
    @pl.when(pl.program_id(1) == 0)
    def _():
        u_ref[...] = _rms(h_ref[...], g_ref[...]).astype(BF16)

    res = _dot(u_ref[...], w_ref[...])
    for c in range(o_ref.shape[0]):
        o_ref[c] = res[:, c * LANE:(c + 1) * LANE].astype(o_ref.dtype)


def _proj_heads(h, g, w, col0, N, dtype, tm, tn):
    S, D = h.shape
    j0 = col0 // tn
    return pl.pallas_call(
        _proj_heads_kernel,
        grid=(S // tm, N // tn),
        in_specs=[
            pl.BlockSpec((tm, D), lambda i, j: (i, 0)),
            pl.BlockSpec((1, D), lambda i, j: (0, 0)),
            pl.BlockSpec((D, tn), lambda i, j: (0, j0 + j)),
        ],
        out_specs=pl.BlockSpec((tn // LANE, tm, LANE), lambda i, j: (j, i, 0)),
        out_shape=jax.ShapeDtypeStruct((N // LANE, S, LANE), dtype),
        scratch_shapes=[pltpu.VMEM((tm, D), BF16)],
        compiler_params=_cparams("parallel", "arbitrary"),
        name="proj_heads",
    )(h, g, w)


def _proj_rows_kernel(h_ref, g_ref, w_ref, wt_ref, o_ref, ot_ref, u_ref):
    j = pl.program_id(1)

    @pl.when(j == 0)
    def _():
        u_ref[...] = _rms(h_ref[...], g_ref[...]).astype(BF16)
        ot_ref[...] = _dot_nt(wt_ref[...], u_ref[...])

    o_ref[...] = _dot(u_ref[...], w_ref[...])


def _proj_rows(h, g, w, col0, N, wt, tm, tn):
    S, D = h.shape
    j0 = col0 // tn
    R = wt.shape[0]
    return pl.pallas_call(
        _proj_rows_kernel,
        grid=(S // tm, N // tn),
        in_specs=[
            pl.BlockSpec((tm, D), lambda i, j: (i, 0)),
            pl.BlockSpec((1, D), lambda i, j: (0, 0)),
            pl.BlockSpec((D, tn), lambda i, j: (0, j0 + j)),
            pl.BlockSpec((R, D), lambda i, j: (0, 0)),
        ],
        out_specs=[
            pl.BlockSpec((tm, tn), lambda i, j: (i, j)),
            pl.BlockSpec((R, tm), lambda i, j: (0, i)),
        ],
        out_shape=[jax.ShapeDtypeStruct((S, N), F32), jax.ShapeDtypeStruct((R, S), F32)],
        scratch_shapes=[pltpu.VMEM((tm, D), BF16)],
        compiler_params=_cparams("parallel", "arbitrary"),
        name="proj_rows",
    )(h, g, w, wt)


def _dn_kernel(alog_ref, dtb_ref,
               q_ref, k_ref, v_ref, z_ref, b_ref, a_ref,
               cw_ref, gn_ref, o_ref,
               state_ref, tail_ref, xbuf_ref, q_s, k_s, kb_s, rhs_s, qg_s, kg_s,
               decay_s, egl_s, qt_s, o0_s, kw_s, bb_s):
    HG, TB = q_ref.shape[0], q_ref.shape[1]
    C = CHUNK
    NC = TB // C
    h0 = pl.program_id(0) * HG

    @pl.when(pl.program_id(1) == 0)
    def _():
        state_ref[...] = jnp.zeros_like(state_ref)
        tail_ref[...] = jnp.zeros_like(tail_ref)

    ri = lax.broadcasted_iota(jnp.int32, (TB, C), 0) & (C - 1)
    ci = lax.broadcasted_iota(jnp.int32, (TB, C), 1)
    lower = ri >= ci
    ri1 = lax.broadcasted_iota(jnp.int32, (C, C), 0)
    ci1 = lax.broadcasted_iota(jnp.int32, (C, C), 1)
    lower1 = ri1 >= ci1
    strict1 = ri1 > ci1
    eye_f = (ri1 == ci1).astype(F32)
    ones_l = jnp.ones((C, LANE), BF16)
    ones_2 = jnp.ones((2 * C, LANE), BF16)
    upper01 = (ri1 <= ci1).astype(BF16)

    def l2n(t):
        return t * lax.rsqrt(jnp.sum(t * t, axis=-1, keepdims=True) + L2_EPS)

    def rep(m):
        return jnp.concatenate([jnp.broadcast_to(m[c:c + 1, :], (C, C)) for c in range(NC)], axis=0)

    for g in range(HG):
        def conv_silu(x_ref, idx):
            w = cw_ref[g, idx]
            xbuf_ref[0:8, :] = tail_ref[g, idx]
            xbuf_ref[8:8 + TB, :] = x_ref[g]
            tail_ref[g, idx] = x_ref[g, TB - 8:TB, :]
            y = (w[0:1, :] * xbuf_ref[5:5 + TB, :] + w[1:2, :] * xbuf_ref[6:6 + TB, :]
                 + w[2:3, :] * xbuf_ref[7:7 + TB, :] + w[3:4, :] * xbuf_ref[8:8 + TB, :])
            return _silu(y)

        q = l2n(conv_silu(q_ref, 0)) * (HEAD_DIM ** -0.5)
        k = l2n(conv_silu(k_ref, 1))
        v = conv_silu(v_ref, 2)

        beta2 = _sigmoid(b_ref[g])
        g2 = -jnp.exp(alog_ref[h0 + g]) * _softplus(a_ref[g] + dtb_ref[h0 + g])
        gc2 = _dot_01(g2, upper01)
        gcrow = rep(gc2)
        glast = jnp.concatenate(
            [jnp.broadcast_to(gc2[c:c + 1, C - 1:C], (C, LANE)) for c in range(NC)], axis=0)
        gccol = _dot_01(jnp.where(lower, rep(g2), 0.0), ones_l)
        betacol = _dot(jnp.concatenate(_split2(jnp.where(ri == ci, rep(beta2), 0.0)), axis=1), ones_2)
        dd = jnp.where(lower, gccol[:, :C] - gcrow, 0.0)
        decay_s[g] = jnp.where(lower, jnp.exp(dd), 0.0)
        egc = jnp.exp(gccol)
        kb = k * betacol
        q_s[g] = q
        k_s[g] = k
        kb_s[g] = kb
        rhs_s[g, :, :HEAD_DIM] = v * betacol
        rhs_s[g, :, HEAD_DIM:] = kb * egc
        qg_s[g] = q * egc
        kg_s[g] = k * jnp.exp(glast - gccol)
        egl_s[g] = jnp.exp(glast)

    chains = [(g, c, slice(c * C, (c + 1) * C)) for g in range(HG) for c in range(NC)]
    xs, attns = [], []
    for g, c, rows in chains:
        kc16 = k_s[g, rows, :].astype(BF16)
        lhs = jnp.concatenate([kb_s[g, rows, :], q_s[g, rows, :]], axis=0).astype(BF16)
        pq = _dot_nt(lhs, kc16)
        dec = decay_s[g, rows, :]
        xs.append(-jnp.where(strict1, pq[:C] * dec, 0.0))
        attns.append(jnp.where(lower1, pq[C:] * dec, 0.0).astype(BF16))
    ts = [eye_f + x for x in xs]
    ps = [x.astype(BF16) for x in xs]
    for _ in range(5):
        ps = [_dot(p, p).astype(BF16) for p in ps]
        ts = [t + _dot(t.astype(BF16), p) for t, p in zip(ts, ps)]
    rs = []
    for x, t in zip(xs, ts):
        xh, xl = _split2(x)
        th, tl = _split2(t)
        rs.append((eye_f - t) + _dot_split(xh, xl, th, tl))
    ts = [t + _dot(t.astype(BF16), r.astype(BF16)) for t, r in zip(ts, rs)]
    sols = []
    for (g, c, rows), t in zip(chains, ts):
        th, tl = _split2(t)
        rh, rl = _split2(rhs_s[g, rows, :])
        sols.append(_dot_split(th, tl, rh, rl).astype(BF16))
    for (g, c, rows), attn, sol16 in zip(chains, attns, sols):
        aw = _dot(attn, sol16)
        o0_s[g, rows, :] = aw[:, :HEAD_DIM]
        qt_s[g, rows, :] = (qg_s[g, rows, :] - aw[:, HEAD_DIM:]).astype(BF16)
    for (g, c, rows), sol16 in zip(chains, sols):
        kwu = _dot_tn(kg_s[g, rows, :].astype(BF16), sol16)
        bb_s[g, c] = kwu[:, :HEAD_DIM]
        kw_s[g, c] = kwu[:, HEAD_DIM:].astype(BF16)

    gn = gn_ref[...]
    sts = [state_ref[g] for g in range(HG)]
    for c in range(NC):
        rows = slice(c * C, (c + 1) * C)
        st16 = [st.astype(BF16) for st in sts]
        sts = [sts[g] * egl_s[g, c * C:c * C + 1, :] - _dot(kw_s[g, c], st16[g]) + bb_s[g, c]
               for g in range(HG)]
        for g in range(HG):
            o = _dot(qt_s[g, rows, :], st16[g]) + o0_s[g, rows, :]
            on = o * lax.rsqrt(jnp.mean(o * o, axis=-1, keepdims=True) + NORM_EPS) * gn
            o_ref[rows, g * HEAD_DIM:(g + 1) * HEAD_DIM] = (on * _silu(z_ref[g, rows, :])).astype(o_ref.dtype)
    for g in range(HG):
        state_ref[g] = sts[g]


def _deltanet(dnz, ba, conv_w, a_log, dt_bias, gnorm, tb, hg):
    H = DN_HEADS
    S = dnz.shape[1]
    NC = tb // CHUNK
    nh = H // hg
    grid_spec = pltpu.PrefetchScalarGridSpec(
        num_scalar_prefetch=2,
        grid=(nh, S // tb),
        in_specs=[
            pl.BlockSpec((hg, tb, LANE), lambda h, t, *_: (h, t, 0)),
            pl.BlockSpec((hg, tb, LANE), lambda h, t, *_: (nh + h, t, 0)),
            pl.BlockSpec((hg, tb, LANE), lambda h, t, *_: (2 * nh + h, t, 0)),
            pl.BlockSpec((hg, tb, LANE), lambda h, t, *_: (3 * nh + h, t, 0)),
            pl.BlockSpec((hg, NC, CHUNK), lambda h, t, *_: (h, t, 0)),
            pl.BlockSpec((hg, NC, CHUNK), lambda h, t, *_: (nh + h, t, 0)),
            pl.BlockSpec((hg, 3, DN_CONV, LANE), lambda h, t, *_: (h, 0, 0, 0)),
            pl.BlockSpec((1, LANE), lambda h, t, *_: (0, 0)),
        ],
        out_specs=pl.BlockSpec((tb, hg * LANE), lambda h, t, *_: (t, h)),
        scratch_shapes=[
            pltpu.VMEM((hg, HEAD_DIM, HEAD_DIM), F32),
            pltpu.VMEM((hg, 3, 8, LANE), F32),
            pltpu.VMEM((tb + 8, LANE), F32),
            pltpu.VMEM((hg, tb, LANE), F32),
            pltpu.VMEM((hg, tb, LANE), F32),
            pltpu.VMEM((hg, tb, LANE), F32),
            pltpu.VMEM((hg, tb, 2 * LANE), F32),
            pltpu.VMEM((hg, tb, LANE), F32),
            pltpu.VMEM((hg, tb, LANE), F32),
            pltpu.VMEM((hg, tb, CHUNK), F32),
            pltpu.VMEM((hg, tb, LANE), F32),
            pltpu.VMEM((hg, tb, LANE), BF16),
            pltpu.VMEM((hg, tb, LANE), F32),
            pltpu.VMEM((hg, NC, HEAD_DIM, HEAD_DIM), BF16),
            pltpu.VMEM((hg, NC, HEAD_DIM, HEAD_DIM), F32),
        ],
    )
    return pl.pallas_call(
        _dn_kernel,
        grid_spec=grid_spec,
        out_shape=jax.ShapeDtypeStruct((S, H * HEAD_DIM), BF16),
        compiler_params=_cparams("parallel", "arbitrary"),
        name="deltanet",
    )(a_log, dt_bias, dnz, dnz, dnz, dnz, ba, ba, conv_w, gnorm)


SB_T = 128
SB_NEAR = 3


def _sb_kernel(q_ref, k_ref, v_ref, o_ref, acc_s, c_s):
    T = SB_T
    QB = q_ref.shape[1] // T
    qb0 = pl.program_id(1) * QB
    scale = HEAD_DIM ** -0.5
    rq = lax.broadcasted_iota(jnp.int32, (T, T), 0)
    ck = lax.broadcasted_iota(jnp.int32, (T, T), 1)
    diag_mask = ck < rq
    later_ones = jnp.concatenate([(rq > ck).astype(BF16), jnp.ones((T, T), BF16)], axis=1)

    def tiles(qs, kbs, masks, cs):
        kbc = [jnp.maximum(kb, 0) for kb in kbs]
        starts = [pl.multiple_of(kb * T, T) for kb in kbc]
        zs = [_dot_nt(q, k_ref[0, pl.ds(s0, T), :]) * scale for q, s0 in zip(qs, starts)]
        lss, l1ms = [], []
        for z, m in zip(zs, masks):
            ls = jnp.minimum(z, 0.0) - jnp.log(1.0 + jnp.exp(-jnp.abs(z)))
            lss.append(ls)
            l1ms.append(jnp.where(m, ls - z, 0.0))
        sums = []
        for l1m in l1ms:
            hi, lo = _split2(l1m)
            sums.append(_dot(hi, later_ones) + _dot(lo, later_ones))
        outs, new_cs = [], []
        for ls, m, sm, c, s0 in zip(lss, masks, sums, cs, starts):
            a = jnp.where(m, jnp.exp(ls + sm[:, :T] + c), 0.0)
            outs.append(_dot(a.astype(BF16), v_ref[0, pl.ds(s0, T), :]))
            new_cs.append(c + sm[:, T:])
        return outs, new_cs

    qs = [q_ref[0, i * T:(i + 1) * T, :] for i in range(QB)]
    cs = [jnp.zeros((T, T), F32) for _ in range(QB)]
    accs = [jnp.zeros((T, HEAD_DIM), F32) for _ in range(QB)]
    for j in range(SB_NEAR):
        kbs = [qb0 + i - j for i in range(QB)]
        masks = [diag_mask if j == 0 else jnp.broadcast_to(kb >= 0, (T, T)) for kb in kbs]
        outs, cs = tiles(qs, kbs, masks, cs)
        accs = [a + o for a, o in zip(accs, outs)]
    for i in range(QB):
        acc_s[i] = accs[i]
        c_s[i] = cs[i]
    cmax = functools.reduce(jnp.maximum, [jnp.max(c) for c in cs])

    def cond(carry):
        t, cm = carry
        return jnp.logical_and(qb0 + QB - 1 - SB_NEAR - t >= 0, cm > SB_SKIP_LOG)

    def body(carry):
        t, _ = carry
        kbs = [qb0 + i - SB_NEAR - t for i in range(QB)]
        masks = [jnp.broadcast_to(kb >= 0, (T, T)) for kb in kbs]
        outs, new_cs = tiles(qs, kbs, masks, [c_s[i] for i in range(QB)])
        for i in range(QB):
            acc_s[i] += outs[i]
            c_s[i] = new_cs[i]
        return t + 1, functools.reduce(jnp.maximum, [jnp.max(c) for c in new_cs])

    lax.while_loop(cond, body, (jnp.int32(0), cmax))
    for i in range(QB):
        o_ref[i * T:(i + 1) * T, :] = acc_s[i].astype(o_ref.dtype)


def _stick_breaking(sb, tq):
    H = SB_HEADS
    S = sb.shape[1]
    return pl.pallas_call(
        _sb_kernel,
        grid=(H, S // tq),
        in_specs=[
            pl.BlockSpec((1, tq, LANE), lambda h, i: (h, i, 0)),
            pl.BlockSpec((1, S, LANE), lambda h, i: (H + h, 0, 0)),
            pl.BlockSpec((1, S, LANE), lambda h, i: (2 * H + h, 0, 0)),
        ],
        out_specs=pl.BlockSpec((tq, LANE), lambda h, i: (i, h)),
        out_shape=jax.ShapeDtypeStruct((S, H * HEAD_DIM), BF16),
        scratch_shapes=[pltpu.VMEM((tq // SB_T, SB_T, HEAD_DIM), F32),
                        pltpu.VMEM((tq // SB_T, SB_T, SB_T), F32)],
        compiler_params=_cparams("parallel", "arbitrary"),
        name="stick_breaking",
    )(sb, sb, sb)


def _merge_kernel(od_ref, os_ref, gd_ref, gs_ref, wd_ref, ws_ref, o_ref):
    m = (_sigmoid(gd_ref[...]) * _dot(od_ref[...], wd_ref[...])
         + _sigmoid(gs_ref[...]) * _dot(os_ref[...], ws_ref[...]))
    o_ref[...] = m.astype(o_ref.dtype)


def _merge(o_dn, o_sb, gates, w_dn, w_sb, tm, tn):
    S, W = o_dn.shape
    D = w_dn.shape[1]
    nb = D // tn
    return pl.pallas_call(
        _merge_kernel,
        grid=(S // tm, nb),
        in_specs=[
            pl.BlockSpec((tm, W), lambda i, j: (i, 0)),
            pl.BlockSpec((tm, W), lambda i, j: (i, 0)),
            pl.BlockSpec((tm, tn), lambda i, j: (i, j)),
            pl.BlockSpec((tm, tn), lambda i, j: (i, nb + j)),
            pl.BlockSpec((W, tn), lambda i, j: (0, j)),
            pl.BlockSpec((W, tn), lambda i, j: (0, j)),
        ],
        out_specs=pl.BlockSpec((tm, tn), lambda i, j: (i, j)),
        out_shape=jax.ShapeDtypeStruct((S, D), BF16),
        compiler_params=_cparams("parallel", "arbitrary"),
        name="merge",
    )(o_dn, o_sb, gates, gates, w_dn, w_sb)


def _outproj_kernel(m_ref, w_ref, h_ref, g_ref, o_ref):
    o_ref[...] = h_ref[...] + _rms(_dot(m_ref[...], w_ref[...]), g_ref[...])


def _outproj(m, w, h, g, tm):
    S, D = h.shape
    return pl.pallas_call(
        _outproj_kernel,
        grid=(S // tm,),
        in_specs=[
            pl.BlockSpec((tm, D), lambda i: (i, 0)),
            pl.BlockSpec((D, D), lambda i: (0, 0)),
            pl.BlockSpec((tm, D), lambda i: (i, 0)),
            pl.BlockSpec((1, D), lambda i: (0, 0)),
        ],
        out_specs=pl.BlockSpec((tm, D), lambda i: (i, 0)),
        out_shape=jax.ShapeDtypeStruct((S, D), F32),
        compiler_params=_cparams("parallel"),
        name="outproj",
    )(m, w, h, g)


def _ple_kernel(h_ref, p_ref, gpre_ref, wg_ref, wp_ref, gpost_ref, o_ref):
    h = h_ref[...]
    u = _rms(h, gpre_ref[...]).astype(BF16)
    gate = _sigmoid(_dot(u, wg_ref[...]))
    e = _dot(p_ref[...].astype(BF16), wp_ref[...])
    o_ref[...] = h + _rms(gate * e, gpost_ref[...])


def _ple(h, p, gpre, wg, wp, gpost, tm):
    S, D = h.shape
    P = p.shape[1]
    return pl.pallas_call(
        _ple_kernel,
        grid=(S // tm,),
        in_specs=[
            pl.BlockSpec((tm, D), lambda i: (i, 0)),
            pl.BlockSpec((tm, P), lambda i: (i, 0)),
            pl.BlockSpec((1, D), lambda i: (0, 0)),
            pl.BlockSpec((D, D), lambda i: (0, 0)),
            pl.BlockSpec((P, D), lambda i: (0, 0)),
            pl.BlockSpec((1, D), lambda i: (0, 0)),
        ],
        out_specs=pl.BlockSpec((tm, D), lambda i: (i, 0)),
        out_shape=jax.ShapeDtypeStruct((S, D), F32),
        compiler_params=_cparams("parallel"),
        name="ple",
    )(h, p, gpre, wg, wp, gpost)


def _layer(h, p, ffn1_norm_pre, ffn1_w_gate, ffn1_w_up, ffn1_w_down, ffn1_norm_post,
           mix_norm_pre, w_in, dn_conv_w, dn_A_log, dn_dt_bias, dn_out_norm,
           w_branch_dn, w_branch_sb, w_out, mix_norm_post,
           ffn2_norm_pre, ffn2_w_gate, ffn2_w_up, ffn2_w_down, ffn2_norm_post,
           ple_norm_pre, ple_w_gate, ple_w_proj, ple_norm_post):
    S, D = h.shape
    H = DN_HEADS
    dn_w = H * HEAD_DIM
    sb_w = SB_HEADS * HEAD_DIM
    row = lambda g: g.reshape(1, -1).astype(F32)
    bf = lambda w: w.astype(BF16)
    tm = min(512, S)

    h = _ffn(h, row(ffn1_norm_pre), bf(ffn1_w_gate), bf(ffn1_w_up), bf(ffn1_w_down),
             row(ffn1_norm_post), tm, 512)

    o2 = 4 * dn_w
    o4 = o2 + 2 * H
    o5 = o4 + 3 * sb_w
    gpre = row(mix_norm_pre)
    tmp = min(1024, S)
    w_bf = bf(w_in)
    w_b = w_bf[:, o4:]
    dnz = _proj_heads(h, gpre, w_bf, 0, o2, F32, tmp, 1024)
    sb = _proj_heads(h, gpre, w_b, 0, 3 * sb_w, BF16, tmp, 1024)
    gates, ba = _proj_rows(h, gpre, w_b, 3 * sb_w, 2 * D, bf(w_in[:, o2:o4].T), tmp, 1024)

    conv_w = dn_conv_w.reshape(DN_CONV, 3, H, HEAD_DIM).transpose(2, 1, 0, 3)
    o_dn = _deltanet(dnz, ba.reshape(2 * H, S // CHUNK, CHUNK), conv_w,
                     dn_A_log.astype(F32), dn_dt_bias.astype(F32), row(dn_out_norm), min(512, S), 4)
    o_sb = _stick_breaking(sb, min(2048, S))

    merged = _merge(o_dn, o_sb, gates, bf(w_branch_dn), bf(w_branch_sb), tm, 1024)
    h = _outproj(merged, bf(w_out), h, row(mix_norm_post), tm)

    h = _ffn(h, row(ffn2_norm_pre), bf(ffn2_w_gate), bf(ffn2_w_up), bf(ffn2_w_down),
             row(ffn2_norm_post), tm, 512)

    return _ple(h, p, row(ple_norm_pre), bf(ple_w_gate), bf(ple_w_proj), row(ple_norm_post), tm)


def kernel(x, p, ffn1_norm_pre, ffn1_w_gate, ffn1_w_up, ffn1_w_down, ffn1_norm_post, mix_norm_pre, w_in, dn_conv_w, dn_A_log, dn_dt_bias, dn_out_norm, w_branch_dn, w_branch_sb, w_out, mix_norm_post, ffn2_norm_pre, ffn2_w_gate, ffn2_w_up, ffn2_w_down, ffn2_norm_post, ple_norm_pre, ple_w_gate, ple_w_proj, ple_norm_post):
    B, S, D = x.shape
    depth = w_in.shape[0]
    outs = []
    for b in range(B):
        h = x[b]
        for i in range(depth):
            h = _layer(h, p[i, b], ffn1_norm_pre[i], ffn1_w_gate[i], ffn1_w_up[i], ffn1_w_down[i],
                       ffn1_norm_post[i], mix_norm_pre[i], w_in[i], dn_conv_w[i], dn_A_log[i],
                       dn_dt_bias[i], dn_out_norm[i], w_branch_dn[i], w_branch_sb[i], w_out[i],
                       mix_norm_post[i], ffn2_norm_pre[i], ffn2_w_gate[i], ffn2_w_up[i],
                       ffn2_w_down[i], ffn2_norm_post[i], ple_norm_pre[i], ple_w_gate[i],
                       ple_w_proj[i], ple_norm_post[i])
        outs.append(h)
    return outs[0].reshape(B, S, D) if B == 1 else jnp.stack(outs)
```

```python
import functools

import jax
import jax.numpy as jnp
from jax import lax
from jax.experimental import pallas as pl
from jax.experimental.pallas import tpu as pltpu

NORM_EPS = 1e-6
L2_EPS = 1e-6
CHUNK = 64
DN_HEADS = 16
SB_HEADS = 16
HEAD_DIM = 128
DN_CONV = 4
LANE = 128
SB_SKIP_LOG = -104.0
VMEM_LIMIT = 56 * 1024 * 1024

F32 = jnp.float32
BF16 = jnp.bfloat16


def _cparams(*sem):
    return pltpu.CompilerParams(dimension_semantics=sem, vmem_limit_bytes=VMEM_LIMIT)


def _rms(x, g):
    return x * lax.rsqrt(jnp.mean(x * x, axis=-1, keepdims=True) + NORM_EPS) * g


def _sigmoid(x):
    return 0.5 * jnp.tanh(0.5 * x) + 0.5


def _silu(x):
    h = 0.5 * x
    return h + h * jnp.tanh(h)


def _softplus(x):
    return jnp.maximum(x, 0.0) + jnp.log(1.0 + jnp.exp(-jnp.abs(x)))


def _dot(a, b):
    return jnp.dot(a, b, preferred_element_type=F32)


def _dot_nt(a, b):
    return lax.dot_general(a, b, (((1,), (1,)), ((), ())), preferred_element_type=F32)


def _dot_tn(a, b):
    return lax.dot_general(a, b, (((0,), (0,)), ((), ())), preferred_element_type=F32)


def _split2(a):
    hi = a.astype(BF16)
    return hi, (a - hi.astype(F32)).astype(BF16)


def _split3(a):
    p1 = a.astype(BF16)
    r = a - p1.astype(F32)
    p2 = r.astype(BF16)
    return p1, p2, (r - p2.astype(F32)).astype(BF16)


def _dot_split(ah, al, bh, bl):
    return _dot(jnp.concatenate([ah, ah, al], axis=1), jnp.concatenate([bh, bl, bh], axis=0))


def _dot_01(a, b01):
    return _dot(jnp.concatenate(_split3(a), axis=1), jnp.concatenate([b01, b01, b01], axis=0))


def _ffn_kernel(h_ref, gpre_ref, wg_ref, wu_ref, wd_ref, gpost_ref, o_ref, u_ref, acc_ref):
    j = pl.program_id(1)

    @pl.when(j == 0)
    def _():
        u_ref[...] = _rms(h_ref[...], gpre_ref[...]).astype(BF16)
        acc_ref[...] = jnp.zeros_like(acc_ref)

    u = u_ref[...]
    g = _dot(u, wg_ref[...])
    up = _dot(u, wu_ref[...])
    a = (_silu(g) * up).astype(BF16)
    acc_ref[...] += _dot(a, wd_ref[...])

    @pl.when(j == pl.num_programs(1) - 1)
    def _():
        o_ref[...] = h_ref[...] + 0.5 * _rms(acc_ref[...], gpost_ref[...])


def _ffn(h, gpre, wg, wu, wd, gpost, tm, tf):
    S, D = h.shape
    FF = wg.shape[1]
    return pl.pallas_call(
        _ffn_kernel,
        grid=(S // tm, FF // tf),
        in_specs=[
            pl.BlockSpec((tm, D), lambda i, j: (i, 0)),
            pl.BlockSpec((1, D), lambda i, j: (0, 0)),
            pl.BlockSpec((D, tf), lambda i, j: (0, j)),
            pl.BlockSpec((D, tf), lambda i, j: (0, j)),
            pl.BlockSpec((tf, D), lambda i, j: (j, 0)),
            pl.BlockSpec((1, D), lambda i, j: (0, 0)),
        ],
        out_specs=pl.BlockSpec((tm, D), lambda i, j: (i, 0)),
        out_shape=jax.ShapeDtypeStruct((S, D), F32),
        scratch_shapes=[pltpu.VMEM((tm, D), BF16), pltpu.VMEM((tm, D), F32)],
        compiler_params=_cparams("parallel", "arbitrary"),
        name="ffn",
    )(h, gpre, wg, wu, wd, gpost)


def _proj_heads_kernel(h_ref, g_ref, w_ref, o_ref, u_ref):
    @pl.when(pl.program_id(1) == 0)
    def _():
        u_ref[...] = _rms(h_ref[...], g_ref[...]).astype(BF16)

    res = _dot(u_ref[...], w_ref[...])
    for c in range(o_ref.shape[0]):
        o_ref[c] = res[:, c * LANE:(c + 1) * LANE].astype(o_ref.dtype)


def _proj_heads(h, g, w, col0, N, dtype, tm, tn):
    S, D = h.shape
    j0 = col0 // tn
    return pl.pallas_call(
        _proj_heads_kernel,
        grid=(S // tm, N // tn),
        in_specs=[
            pl.BlockSpec((tm, D), lambda i, j: (i, 0)),
            pl.BlockSpec((1, D), lambda i, j: (0, 0)),
            pl.BlockSpec((D, tn), lambda i, j: (0, j0 + j)),
        ],
        out_specs=pl.BlockSpec((tn // LANE, tm, LANE), lambda i, j: (j, i, 0)),
        out_shape=jax.ShapeDtypeStruct((N // LANE, S, LANE), dtype),
        scratch_shapes=[pltpu.VMEM((tm, D), BF16)],
        compiler_params=_cparams("parallel", "arbitrary"),
        name="proj_heads",
    )(h, g, w)


def _proj_rows_kernel(h_ref, g_ref, w_ref, wt_ref, o_ref, ot_ref, u_ref):
    j = pl.program_id(1)

    @pl.when(j == 0)
    def _():
        u_ref[...] = _rms(h_ref[...], g_ref[...]).astype(BF16)
        ot_ref[...] = _dot_nt(wt_ref[...], u_ref[...])

    o_ref[...] = _dot(u_ref[...], w_ref[...])


def _proj_rows(h, g, w, col0, N, wt, tm, tn):
    S, D = h.shape
    j0 = col0 // tn
    R = wt.shape[0]
    return pl.pallas_call(
        _proj_rows_kernel,
        grid=(S // tm, N // tn),
        in_specs=[
            pl.BlockSpec((tm, D), lambda i, j: (i, 0)),
            pl.BlockSpec((1, D), lambda i, j: (0, 0)),
            pl.BlockSpec((D, tn), lambda i, j: (0, j0 + j)),
            pl.BlockSpec((R, D), lambda i, j: (0, 0)),
        ],
        out_specs=[
            pl.BlockSpec((tm, tn), lambda i, j: (i, j)),
            pl.BlockSpec((R, tm), lambda i, j: (0, i)),
        ],
        out_shape=[jax.ShapeDtypeStruct((S, N), F32), jax.ShapeDtypeStruct((R, S), F32)],
        scratch_shapes=[pltpu.VMEM((tm, D), BF16)],
        compiler_params=_cparams("parallel", "arbitrary"),
        name="proj_rows",
    )(h, g, w, wt)


def _dn_kernel(alog_ref, dtb_ref,
               q_ref, k_ref, v_ref, z_ref, b_ref, a_ref,
               cw_ref, gn_ref, o_ref,
               state_ref, tail_ref, xbuf_ref, q_s, k_s, kb_s, rhs_s, qg_s, kg_s,
               decay_s, egl_s, qt_s, o0_s, kw_s, bb_s):
    HG, TB = q_ref.shape[0], q_ref.shape[1]
    C = CHUNK
    NC = TB // C
    h0 = pl.program_id(0) * HG

    @pl.when(pl.program_id(1) == 0)
    def _():
        state_ref[...] = jnp.zeros_like(state_ref)
        tail_ref[...] = jnp.zeros_like(tail_ref)

    ri = lax.broadcasted_iota(jnp.int32, (TB, C), 0) & (C - 1)
    ci = lax.broadcasted_iota(jnp.int32, (TB, C), 1)
    lower = ri >= ci
    ri1 = lax.broadcasted_iota(jnp.int32, (C, C), 0)
    ci1 = lax.broadcasted_iota(jnp.int32, (C, C), 1)
    lower1 = ri1 >= ci1
    strict1 = ri1 > ci1
    eye_f = (ri1 == ci1).astype(F32)
    ones_l = jnp.ones((C, LANE), BF16)
    ones_2 = jnp.ones((2 * C, LANE), BF16)
    upper01 = (ri1 <= ci1).astype(BF16)

    def l2n(t):
        return t * lax.rsqrt(jnp.sum(t * t, axis=-1, keepdims=True) + L2_EPS)

    def rep(m):
        return jnp.concatenate([jnp.broadcast_to(m[c:c + 1, :], (C, C)) for c in range(NC)], axis=0)

    for g in range(HG):
        def conv_silu(x_ref, idx):
            w = cw_ref[g, idx]
            xbuf_ref[0:8, :] = tail_ref[g, idx]
            xbuf_ref[8:8 + TB, :] = x_ref[g]
            tail_ref[g, idx] = x_ref[g, TB - 8:TB, :]
            y = (w[0:1, :] * xbuf_ref[5:5 + TB, :] + w[1:2, :] * xbuf_ref[6:6 + TB, :]
                 + w[2:3, :] * xbuf_ref[7:7 + TB, :] + w[3:4, :] * xbuf_ref[8:8 + TB, :])
            return _silu(y)

        q = l2n(conv_silu(q_ref, 0)) * (HEAD_DIM ** -0.5)
        k = l2n(conv_silu(k_ref, 1))
        v = conv_silu(v_ref, 2)

        beta2 = _sigmoid(b_ref[g])
        g2 = -jnp.exp(alog_ref[h0 + g]) * _softplus(a_ref[g] + dtb_ref[h0 + g])
        gc2 = _dot_01(g2, upper01)
        gcrow = rep(gc2)
        glast = jnp.concatenate(
            [jnp.broadcast_to(gc2[c:c + 1, C - 1:C], (C, LANE)) for c in range(NC)], axis=0)
        gccol = _dot_01(jnp.where(lower, rep(g2), 0.0), ones_l)
        betacol = _dot(jnp.concatenate(_split2(jnp.where(ri == ci, rep(beta2), 0.0)), axis=1), ones_2)
        dd = jnp.where(lower, gccol[:, :C] - gcrow, 0.0)
        decay_s[g] = jnp.where(lower, jnp.exp(dd), 0.0)
        egc = jnp.exp(gccol)
        kb = k * betacol
        q_s[g] = q
        k_s[g] = k
        kb_s[g] = kb
        rhs_s[g, :, :HEAD_DIM] = v * betacol
        rhs_s[g, :, HEAD_DIM:] = kb * egc
        qg_s[g] = q * egc
        kg_s[g] = k * jnp.exp(glast - gccol)
        egl_s[g] = jnp.exp(glast)

    chains = [(g, c, slice(c * C, (c + 1) * C)) for g in range(HG) for c in range(NC)]
    xs, attns = [], []
    for g, c, rows in chains:
        kc16 = k_s[g, rows, :].astype(BF16)
        lhs = jnp.concatenate([kb_s[g, rows, :], q_s[g, rows, :]], axis=0).astype(BF16)
        pq = _dot_nt(lhs, kc16)
        dec = decay_s[g, rows, :]
        xs.append(-jnp.where(strict1, pq[:C] * dec, 0.0))
        attns.append(jnp.where(lower1, pq[C:] * dec, 0.0).astype(BF16))
    ts = [eye_f + x for x in xs]
    ps = [x.astype(BF16) for x in xs]
    for _ in range(5):
        ps = [_dot(p, p).astype(BF16) for p in ps]
        ts = [t + _dot(t.astype(BF16), p) for t, p in zip(ts, ps)]
    rs = []
    for x, t in zip(xs, ts):
        xh, xl = _split2(x)
        th, tl = _split2(t)
        rs.append((eye_f - t) + _dot_split(xh, xl, th, tl))
    ts = [t + _dot(t.astype(BF16), r.astype(BF16)) for t, r in zip(ts, rs)]
    sols = []
    for (g, c, rows), t in zip(chains, ts):
        th, tl = _split2(t)
        rh, rl = _split2(rhs_s[g, rows, :])
        sols.append(_dot_split(th, tl, rh, rl).astype(BF16))
    for (g, c, rows), attn, sol16 in zip(chains, attns, sols):
        aw = _dot(attn, sol16)
        o0_s[g, rows, :] = aw[:, :HEAD_DIM]
        qt_s[g, rows, :] = (qg_s[g, rows, :] - aw[:, HEAD_DIM:]).astype(BF16)
    for (g, c, rows), sol16 in zip(chains, sols):
        kwu = _dot_tn(kg_s[g, rows, :].astype(BF16), sol16)
        bb_s[g, c] = kwu[:, :HEAD_DIM]
        kw_s[g, c] = kwu[:, HEAD_DIM:].astype(BF16)

    gn = gn_ref[...]
    sts = [state_ref[g] for g in range(HG)]
    for c in range(NC):
        rows = slice(c * C, (c + 1) * C)
        st16 = [st.astype(BF16) for st in sts]
        sts = [sts[g] * egl_s[g, c * C:c * C + 1, :] - _dot(kw_s[g, c], st16[g]) + bb_s[g, c]
               for g in range(HG)]
        for g in range(HG):
            o = _dot(qt_s[g, rows, :], st16[g]) + o0_s[g, rows, :]
            on = o * lax.rsqrt(jnp.mean(o * o, axis=-1, keepdims=True) + NORM_EPS) * gn
            o_ref[rows, g * HEAD_DIM:(g + 1) * HEAD_DIM] = (on * _silu(z_ref[g, rows, :])).astype(o_ref.dtype)
    for g in range(HG):
        state_ref[g] = sts[g]


def _deltanet(dnz, ba, conv_w, a_log, dt_bias, gnorm, tb, hg):
    H = DN_HEADS
    S = dnz.shape[1]
    NC = tb // CHUNK
    nh = H // hg
    grid_spec = pltpu.PrefetchScalarGridSpec(
        num_scalar_prefetch=2,
        grid=(nh, S // tb),
        in_specs=[
            pl.BlockSpec((hg, tb, LANE), lambda h, t, *_: (h, t, 0)),
            pl.BlockSpec((hg, tb, LANE), lambda h, t, *_: (nh + h, t, 0)),
            pl.BlockSpec((hg, tb, LANE), lambda h, t, *_: (2 * nh + h, t, 0)),
            pl.BlockSpec((hg, tb, LANE), lambda h, t, *_: (3 * nh + h, t, 0)),
            pl.BlockSpec((hg, NC, CHUNK), lambda h, t, *_: (h, t, 0)),
            pl.BlockSpec((hg, NC, CHUNK), lambda h, t, *_: (nh + h, t, 0)),
            pl.BlockSpec((hg, 3, DN_CONV, LANE), lambda h, t, *_: (h, 0, 0, 0)),
            pl.BlockSpec((1, LANE), lambda h, t, *_: (0, 0)),
        ],
        out_specs=pl.BlockSpec((tb, hg * LANE), lambda h, t, *_: (t, h)),
        scratch_shapes=[
            pltpu.VMEM((hg, HEAD_DIM, HEAD_DIM), F32),
            pltpu.VMEM((hg, 3, 8, LANE), F32),
            pltpu.VMEM((tb + 8, LANE), F32),
            pltpu.VMEM((hg, tb, LANE), F32),
            pltpu.VMEM((hg, tb, LANE), F32),
            pltpu.VMEM((hg, tb, LANE), F32),
            pltpu.VMEM((hg, tb, 2 * LANE), F32),
            pltpu.VMEM((hg, tb, LANE), F32),
            pltpu.VMEM((hg, tb, LANE), F32),
            pltpu.VMEM((hg, tb, CHUNK), F32),
            pltpu.VMEM((hg, tb, LANE), F32),
            pltpu.VMEM((hg, tb, LANE), BF16),
            pltpu.VMEM((hg, tb, LANE), F32),
            pltpu.VMEM((hg, NC, HEAD_DIM, HEAD_DIM), BF16),
            pltpu.VMEM((hg, NC, HEAD_DIM, HEAD_DIM), F32),
        ],
    )
    return pl.pallas_call(
        _dn_kernel,
        grid_spec=grid_spec,
        out_shape=jax.ShapeDtypeStruct((S, H * HEAD_DIM), BF16),
        compiler_params=_cparams("parallel", "arbitrary"),
        name="deltanet",
    )(a_log, dt_bias, dnz, dnz, dnz, dnz, ba, ba, conv_w, gnorm)


SB_T = 128
SB_NEAR = 3


def _sb_kernel(q_ref, k_ref, v_ref, o_ref, acc_s, c_s):
    T = SB_T
    QB = q_ref.shape[1] // T
    qb0 = pl.program_id(1) * QB
    scale = HEAD_DIM ** -0.5
    rq = lax.broadcasted_iota(jnp.int32, (T, T), 0)
    ck = lax.broadcasted_iota(jnp.int32, (T, T), 1)
    diag_mask = ck < rq
    later_ones = jnp.concatenate([(rq > ck).astype(BF16), jnp.ones((T, T), BF16)], axis=1)
    later_ones2 = jnp.concatenate([later_ones, later_ones], axis=0)

    def tiles(qs, kbs, masks, cs):
        kbc = [jnp.maximum(kb, 0) for kb in kbs]
        starts = [pl.multiple_of(kb * T, T) for kb in kbc]
        zs = [_dot_nt(q, k_ref[0, pl.ds(s0, T), :]) * scale for q, s0 in zip(qs, starts)]
        lss, l1ms = [], []
        for z, m in zip(zs, masks):
            ls = jnp.minimum(z, 0.0) - jnp.log(1.0 + jnp.exp(-jnp.abs(z)))
            lss.append(ls)
            l1ms.append(jnp.where(m, ls - z, 0.0))
        sums = []
        for l1m in l1ms:
            sums.append(_dot(jnp.concatenate(_split2(l1m), axis=1), later_ones2))
        outs, new_cs = [], []
        for ls, m, sm, c, s0 in zip(lss, masks, sums, cs, starts):
            a = jnp.where(m, jnp.exp(ls + sm[:, :T] + c), 0.0)
            outs.append(_dot(a.astype(BF16), v_ref[0, pl.ds(s0, T), :]))
            new_cs.append(c + sm[:, T:])
        return outs, new_cs

    qs = [q_ref[0, i * T:(i + 1) * T, :] for i in range(QB)]
    cs = [jnp.zeros((T, T), F32) for _ in range(QB)]
    accs = [jnp.zeros((T, HEAD_DIM), F32) for _ in range(QB)]
    for j in range(SB_NEAR):
        kbs = [qb0 + i - j for i in range(QB)]
        masks = [diag_mask if j == 0 else jnp.broadcast_to(kb >= 0, (T, T)) for kb in kbs]
        outs, cs = tiles(qs, kbs, masks, cs)
        accs = [a + o for a, o in zip(accs, outs)]
    for i in range(QB):
        acc_s[i] = accs[i]
        c_s[i] = cs[i]

    def live_max(cs, t):
        return functools.reduce(jnp.maximum, [
            jnp.where(qb0 + i - SB_NEAR - t >= 0, jnp.max(c), -jnp.inf) for i, c in enumerate(cs)])

    def cond(carry):
        _, cm = carry
        return cm > SB_SKIP_LOG

    def body(carry):
        t, _ = carry
        kbs = [qb0 + i - SB_NEAR - t for i in range(QB)]
        masks = [jnp.broadcast_to(kb >= 0, (T, T)) for kb in kbs]
        outs, new_cs = tiles(qs, kbs, masks, [c_s[i] for i in range(QB)])
        for i in range(QB):
            acc_s[i] += outs[i]
            c_s[i] = new_cs[i]
        return t + 1, live_max(new_cs, t + 1)

    lax.while_loop(cond, body, (jnp.int32(0), live_max(cs, 0)))
    for i in range(QB):
        o_ref[i * T:(i + 1) * T, :] = acc_s[i].astype(o_ref.dtype)


def _stick_breaking(sb, tq):
    H = SB_HEADS
    S = sb.shape[1]
    return pl.pallas_call(
        _sb_kernel,
        grid=(H, S // tq),
        in_specs=[
            pl.BlockSpec((1, tq, LANE), lambda h, i: (h, i, 0)),
            pl.BlockSpec((1, S, LANE), lambda h, i: (H + h, 0, 0)),
            pl.BlockSpec((1, S, LANE), lambda h, i: (2 * H + h, 0, 0)),
        ],
        out_specs=pl.BlockSpec((tq, LANE), lambda h, i: (i, h)),
        out_shape=jax.ShapeDtypeStruct((S, H * HEAD_DIM), BF16),
        scratch_shapes=[pltpu.VMEM((tq // SB_T, SB_T, HEAD_DIM), F32),
                        pltpu.VMEM((tq // SB_T, SB_T, SB_T), F32)],
        compiler_params=_cparams("parallel", "arbitrary"),
        name="stick_breaking",
    )(sb, sb, sb)


def _merge_kernel(od_ref, os_ref, gd_ref, gs_ref, wd_ref, ws_ref, o_ref):
    m = (_sigmoid(gd_ref[...]) * _dot(od_ref[...], wd_ref[...])
         + _sigmoid(gs_ref[...]) * _dot(os_ref[...], ws_ref[...]))
    o_ref[...] = m.astype(o_ref.dtype)


def _merge(o_dn, o_sb, gates, w_dn, w_sb, tm, tn):
    S, W = o_dn.shape
    D = w_dn.shape[1]
    nb = D // tn
    return pl.pallas_call(
        _merge_kernel,
        grid=(nb, S // tm),
        in_specs=[
            pl.BlockSpec((tm, W), lambda j, i: (i, 0)),
            pl.BlockSpec((tm, W), lambda j, i: (i, 0)),
            pl.BlockSpec((tm, tn), lambda j, i: (i, j)),
            pl.BlockSpec((tm, tn), lambda j, i: (i, nb + j)),
            pl.BlockSpec((W, tn), lambda j, i: (0, j)),
            pl.BlockSpec((W, tn), lambda j, i: (0, j)),
        ],
        out_specs=pl.BlockSpec((tm, tn), lambda j, i: (i, j)),
        out_shape=jax.ShapeDtypeStruct((S, D), BF16),
        compiler_params=_cparams("parallel", "parallel"),
        name="merge",
    )(o_dn, o_sb, gates, gates, w_dn, w_sb)


def _outproj_kernel(m_ref, w_ref, h_ref, g_ref, o_ref):
    o_ref[...] = h_ref[...] + _rms(_dot(m_ref[...], w_ref[...]), g_ref[...])


def _outproj(m, w, h, g, tm):
    S, D = h.shape
    return pl.pallas_call(
        _outproj_kernel,
        grid=(S // tm,),
        in_specs=[
            pl.BlockSpec((tm, D), lambda i: (i, 0)),
            pl.BlockSpec((D, D), lambda i: (0, 0)),
            pl.BlockSpec((tm, D), lambda i: (i, 0)),
            pl.BlockSpec((1, D), lambda i: (0, 0)),
        ],
        out_specs=pl.BlockSpec((tm, D), lambda i: (i, 0)),
        out_shape=jax.ShapeDtypeStruct((S, D), F32),
        compiler_params=_cparams("parallel"),
        name="outproj",
    )(m, w, h, g)


def _ple_kernel(h_ref, p_ref, gpre_ref, wg_ref, wp_ref, gpost_ref, o_ref):
    h = h_ref[...]
    u = _rms(h, gpre_ref[...]).astype(BF16)
    gate = _sigmoid(_dot(u, wg_ref[...]))
    e = _dot(p_ref[...].astype(BF16), wp_ref[...])
    o_ref[...] = h + _rms(gate * e, gpost_ref[...])


def _ple(h, p, gpre, wg, wp, gpost, tm):
    S, D = h.shape
    P = p.shape[1]
    return pl.pallas_call(
        _ple_kernel,
        grid=(S // tm,),
        in_specs=[
            pl.BlockSpec((tm, D), lambda i: (i, 0)),
            pl.BlockSpec((tm, P), lambda i: (i, 0)),
            pl.BlockSpec((1, D), lambda i: (0, 0)),
            pl.BlockSpec((D, D), lambda i: (0, 0)),
            pl.BlockSpec((P, D), lambda i: (0, 0)),
            pl.BlockSpec((1, D), lambda i: (0, 0)),
        ],
        out_specs=pl.BlockSpec((tm, D), lambda i: (i, 0)),
        out_shape=jax.ShapeDtypeStruct((S, D), F32),
        compiler_params=_cparams("parallel"),
        name="ple",
    )(h, p, gpre, wg, wp, gpost)


def _layer(h, p, ffn1_norm_pre, ffn1_w_gate, ffn1_w_up, ffn1_w_down, ffn1_norm_post,
           mix_norm_pre, w_in, dn_conv_w, dn_A_log, dn_dt_bias, dn_out_norm,
           w_branch_dn, w_branch_sb, w_out, mix_norm_post,
           ffn2_norm_pre, ffn2_w_gate, ffn2_w_up, ffn2_w_down, ffn2_norm_post,
           ple_norm_pre, ple_w_gate, ple_w_proj, ple_norm_post):
    S, D = h.shape
    H = DN_HEADS
    dn_w = H * HEAD_DIM
    sb_w = SB_HEADS * HEAD_DIM
    row = lambda g: g.reshape(1, -1).astype(F32)
    bf = lambda w: w.astype(BF16)
    tm = min(512, S)

    h = _ffn(h, row(ffn1_norm_pre), bf(ffn1_w_gate), bf(ffn1_w_up), bf(ffn1_w_down),
             row(ffn1_norm_post), tm, 512)

    o2 = 4 * dn_w
    o4 = o2 + 2 * H
    o5 = o4 + 3 * sb_w
    gpre = row(mix_norm_pre)
    tmp = min(1024, S)
    w_bf = bf(w_in)
    w_b = w_bf[:, o4:]
    dnz = _proj_heads(h, gpre, w_bf, 0, o2, F32, tmp, 1024)
    sb = _proj_heads(h, gpre, w_b, 0, 3 * sb_w, BF16, tmp, 1024)
    gates, ba = _proj_rows(h, gpre, w_b, 3 * sb_w, 2 * D, bf(w_in[:, o2:o4].T), tmp, 1024)

    conv_w = dn_conv_w.reshape(DN_CONV, 3, H, HEAD_DIM).transpose(2, 1, 0, 3)
    o_dn = _deltanet(dnz, ba.reshape(2 * H, S // CHUNK, CHUNK), conv_w,
                     dn_A_log.astype(F32), dn_dt_bias.astype(F32), row(dn_out_norm), min(512, S), 4)
    o_sb = _stick_breaking(sb, min(2048, S))

    merged = _merge(o_dn, o_sb, gates, bf(w_branch_dn), bf(w_branch_sb), tm, 1024)
    h = _outproj(merged, bf(w_out), h, row(mix_norm_post), tm)

    h = _ffn(h, row(ffn2_norm_pre), bf(ffn2_w_gate), bf(ffn2_w_up), bf(ffn2_w_down),
             row(ffn2_norm_post), tm, 512)

    return _ple(h, p, row(ple_norm_pre), bf(ple_w_gate), bf(ple_w_proj), row(ple_norm_post), tm)


def kernel(x, p, ffn1_norm_pre, ffn1_w_gate, ffn1_w_up, ffn1_w_down, ffn1_norm_post, mix_norm_pre, w_in, dn_conv_w, dn_A_log, dn_dt_bias, dn_out_norm, w_branch_dn, w_branch_sb, w_out, mix_norm_post, ffn2_norm_pre, ffn2_w_gate, ffn2_w_up, ffn2_w_down, ffn2_norm_post, ple_norm_pre, ple_w_gate, ple_w_proj, ple_norm_post):
    B, S, D = x.shape
    depth = w_in.shape[0]
    outs = []
    for b in range(B):
        h = x[b]
        for i in range(depth):
            h = _layer(h, p[i, b], ffn1_norm_pre[i], ffn1_w_gate[i], ffn1_w_up[i], ffn1_w_down[i],
                       ffn1_norm_post[i], mix_norm_pre[i], w_in[i], dn_conv_w[i], dn_A_log[i],
                       dn_dt_bias[i], dn_out_norm[i], w_branch_dn[i], w_branch_sb[i], w_out[i],
                       mix_norm_post[i], ffn2_norm_pre[i], ffn2_w_gate[i], ffn2_w_up[i],
                       ffn2_w_down[i], ffn2_norm_post[i], ple_norm_pre[i], ple_w_gate[i],
                       ple_w_proj[i], ple_norm_post[i])
        outs.append(h)
    return outs[0].reshape(B, S, D) if B == 1 else jnp.stack(outs)
```

```python
import functools

import jax
import jax.numpy as jnp
from jax import lax
from jax.experimental import pallas as pl
from jax.experimental.pallas import tpu as pltpu

NORM_EPS = 1e-6
L2_EPS = 1e-6
CHUNK = 64
DN_HEADS = 16
SB_HEADS = 16
HEAD_DIM = 128
DN_CONV = 4
LANE = 128
SB_SKIP_LOG = -104.0
VMEM_LIMIT = 56 * 1024 * 1024

F32 = jnp.float32
BF16 = jnp.bfloat16


def _cparams(*sem):
    return pltpu.CompilerParams(dimension_semantics=sem, vmem_limit_bytes=VMEM_LIMIT)


def _rms(x, g):
    return x * lax.rsqrt(jnp.mean(x * x, axis=-1, keepdims=True) + NORM_EPS) * g


def _sigmoid(x):
    return 0.5 * jnp.tanh(0.5 * x) + 0.5


def _silu(x):
    h = 0.5 * x
    return h + h * jnp.tanh(h)


def _softplus(x):
    return jnp.maximum(x, 0.0) + jnp.log(1.0 + jnp.exp(-jnp.abs(x)))


def _dot(a, b):
    return jnp.dot(a, b, preferred_element_type=F32)


def _dot_nt(a, b):
    return lax.dot_general(a, b, (((1,), (1,)), ((), ())), preferred_element_type=F32)


def _dot_tn(a, b):
    return lax.dot_general(a, b, (((0,), (0,)), ((), ())), preferred_element_type=F32)


def _split2(a):
    hi = a.astype(BF16)
    return hi, (a - hi.astype(F32)).astype(BF16)


def _split3(a):
    p1 = a.astype(BF16)
    r = a - p1.astype(F32)
    p2 = r.astype(BF16)
    return p1, p2, (r - p2.astype(F32)).astype(BF16)


def _dot_split(ah, al, bh, bl):
    return _dot(jnp.concatenate([ah, ah, al], axis=1), jnp.concatenate([bh, bl, bh], axis=0))


def _dot_01(a, b01):
    return _dot(jnp.concatenate(_split3(a), axis=1), jnp.concatenate([b01, b01, b01], axis=0))


def _ffn_kernel(h_ref, gpre_ref, wg_ref, wu_ref, wd_ref, gpost_ref, o_ref, u_ref, acc_ref):
    j = pl.program_id(1)

    @pl.when(j == 0)
    def _():
        u_ref[...] = _rms(h_ref[...], gpre_ref[...]).astype(BF16)
        acc_ref[...] = jnp.zeros_like(acc_ref)

    u = u_ref[...]
    g = _dot(u, wg_ref[...])
    up = _dot(u, wu_ref[...])
    a = (_silu(g) * up).astype(BF16)
    acc_ref[...] += _dot(a, wd_ref[...])

    @pl.when(j == pl.num_programs(1) - 1)
    def _():
        o_ref[...] = h_ref[...] + 0.5 * _rms(acc_ref[...], gpost_ref[...])


def _ffn(h, gpre, wg, wu, wd, gpost, tm, tf):
    S, D = h.shape
    FF = wg.shape[1]
    return pl.pallas_call(
        _ffn_kernel,
        grid=(S // tm, FF // tf),
        in_specs=[
            pl.BlockSpec((tm, D), lambda i, j: (i, 0)),
            pl.BlockSpec((1, D), lambda i, j: (0, 0)),
            pl.BlockSpec((D, tf), lambda i, j: (0, j)),
            pl.BlockSpec((D, tf), lambda i, j: (0, j)),
            pl.BlockSpec((tf, D), lambda i, j: (j, 0)),
            pl.BlockSpec((1, D), lambda i, j: (0, 0)),
        ],
        out_specs=pl.BlockSpec((tm, D), lambda i, j: (i, 0)),
        out_shape=jax.ShapeDtypeStruct((S, D), F32),
        scratch_shapes=[pltpu.VMEM((tm, D), BF16), pltpu.VMEM((tm, D), F32)],
        compiler_params=_cparams("parallel", "arbitrary"),
        name="ffn",
    )(h, gpre, wg, wu, wd, gpost)


def _proj_heads_kernel(h_ref, g_ref, w_ref, o_ref, u_ref):
    @pl.when(pl.program_id(1) == 0)
    def _():
        u_ref[...] = _rms(h_ref[...], g_ref[...]).astype(BF16)

    res = _dot(u_ref[...], w_ref[...])
    for c in range(o_ref.shape[0]):
        o_ref[c] = res[:, c * LANE:(c + 1) * LANE].astype(o_ref.dtype)


def _proj_heads(h, g, w, col0, N, dtype, tm, tn):
    S, D = h.shape
    j0 = col0 // tn
    return pl.pallas_call(
        _proj_heads_kernel,
        grid=(S // tm, N // tn),
        in_specs=[
            pl.BlockSpec((tm, D), lambda i, j: (i, 0)),
            pl.BlockSpec((1, D), lambda i, j: (0, 0)),
            pl.BlockSpec((D, tn), lambda i, j: (0, j0 + j)),
        ],
        out_specs=pl.BlockSpec((tn // LANE, tm, LANE), lambda i, j: (j, i, 0)),
        out_shape=jax.ShapeDtypeStruct((N // LANE, S, LANE), dtype),
        scratch_shapes=[pltpu.VMEM((tm, D), BF16)],
        compiler_params=_cparams("parallel", "arbitrary"),
        name="proj_heads",
    )(h, g, w)


def _proj_rows_kernel(h_ref, g_ref, w_ref, wt_ref, o_ref, ot_ref, u_ref):
    j = pl.program_id(1)

    @pl.when(j == 0)
    def _():
        u_ref[...] = _rms(h_ref[...], g_ref[...]).astype(BF16)
        ot_ref[...] = _dot_nt(wt_ref[...], u_ref[...])

    o_ref[...] = _dot(u_ref[...], w_ref[...])


def _proj_rows(h, g, w, col0, N, wt, tm, tn):
    S, D = h.shape
    j0 = col0 // tn
    R = wt.shape[0]
    return pl.pallas_call(
        _proj_rows_kernel,
        grid=(S // tm, N // tn),
        in_specs=[
            pl.BlockSpec((tm, D), lambda i, j: (i, 0)),
            pl.BlockSpec((1, D), lambda i, j: (0, 0)),
            pl.BlockSpec((D, tn), lambda i, j: (0, j0 + j)),
            pl.BlockSpec((R, D), lambda i, j: (0, 0)),
        ],
        out_specs=[
            pl.BlockSpec((tm, tn), lambda i, j: (i, j)),
            pl.BlockSpec((R, tm), lambda i, j: (0, i)),
        ],
        out_shape=[jax.ShapeDtypeStruct((S, N), F32), jax.ShapeDtypeStruct((R, S), F32)],
        scratch_shapes=[pltpu.VMEM((tm, D), BF16)],
        compiler_params=_cparams("parallel", "arbitrary"),
        name="proj_rows",
    )(h, g, w, wt)


def _dn_kernel(alog_ref, dtb_ref,
               q_ref, k_ref, v_ref, z_ref, b_ref, a_ref,
               cw_ref, gn_ref, o_ref,
               state_ref, tail_ref, xbuf_ref, q_s, k_s, kb_s, rhs_s, qg_s, kg_s,
               decay_s, egl_s, qt_s, o0_s, kw_s, bb_s):
    HG, TB = q_ref.shape[0], q_ref.shape[1]
    C = CHUNK
    NC = TB // C
    h0 = pl.program_id(0) * HG

    @pl.when(pl.program_id(1) == 0)
    def _():
        state_ref[...] = jnp.zeros_like(state_ref)
        tail_ref[...] = jnp.zeros_like(tail_ref)

    ri = lax.broadcasted_iota(jnp.int32, (TB, C), 0) & (C - 1)
    ci = lax.broadcasted_iota(jnp.int32, (TB, C), 1)
    lower = ri >= ci
    ri1 = lax.broadcasted_iota(jnp.int32, (C, C), 0)
    ci1 = lax.broadcasted_iota(jnp.int32, (C, C), 1)
    lower1 = ri1 >= ci1
    strict1 = ri1 > ci1
    eye_f = (ri1 == ci1).astype(F32)
    ones_l = jnp.ones((C, LANE), BF16)
    ones_2 = jnp.ones((2 * C, LANE), BF16)
    upper01 = (ri1 <= ci1).astype(BF16)

    def l2n(t):
        return t * lax.rsqrt(jnp.sum(t * t, axis=-1, keepdims=True) + L2_EPS)

    def rep(m):
        return jnp.concatenate([jnp.broadcast_to(m[c:c + 1, :], (C, C)) for c in range(NC)], axis=0)

    for g in range(HG):
        def conv_silu(x_ref, idx):
            w = cw_ref[g, idx]
            xbuf_ref[0:8, :] = tail_ref[g, idx]
            xbuf_ref[8:8 + TB, :] = x_ref[g]
            tail_ref[g, idx] = x_ref[g, TB - 8:TB, :]
            y = (w[0:1, :] * xbuf_ref[5:5 + TB, :] + w[1:2, :] * xbuf_ref[6:6 + TB, :]
                 + w[2:3, :] * xbuf_ref[7:7 + TB, :] + w[3:4, :] * xbuf_ref[8:8 + TB, :])
            return _silu(y)

        q = l2n(conv_silu(q_ref, 0)) * (HEAD_DIM ** -0.5)
        k = l2n(conv_silu(k_ref, 1))
        v = conv_silu(v_ref, 2)

        beta2 = _sigmoid(b_ref[g])
        g2 = -jnp.exp(alog_ref[h0 + g]) * _softplus(a_ref[g] + dtb_ref[h0 + g])
        gc2 = _dot_01(g2, upper01)
        gcrow = rep(gc2)
        glast = jnp.concatenate(
            [jnp.broadcast_to(gc2[c:c + 1, C - 1:C], (C, LANE)) for c in range(NC)], axis=0)
        gccol = _dot_01(jnp.where(lower, rep(g2), 0.0), ones_l)
        betacol = _dot(jnp.concatenate(_split2(jnp.where(ri == ci, rep(beta2), 0.0)), axis=1), ones_2)
        dd = jnp.where(lower, gccol[:, :C] - gcrow, 0.0)
        decay_s[g] = jnp.where(lower, jnp.exp(dd), 0.0)
        egc = jnp.exp(gccol)
        kb = k * betacol
        q_s[g] = q
        k_s[g] = k
        kb_s[g] = kb
        rhs_s[g, :, :HEAD_DIM] = v * betacol
        rhs_s[g, :, HEAD_DIM:] = kb * egc
        qg_s[g] = q * egc
        kg_s[g] = k * jnp.exp(glast - gccol)
        egl_s[g] = jnp.exp(glast)

    chains = [(g, c, slice(c * C, (c + 1) * C)) for g in range(HG) for c in range(NC)]
    xs, attns = [], []
    for g, c, rows in chains:
        kc16 = k_s[g, rows, :].astype(BF16)
        lhs = jnp.concatenate([kb_s[g, rows, :], q_s[g, rows, :]], axis=0).astype(BF16)
        pq = _dot_nt(lhs, kc16)
        dec = decay_s[g, rows, :]
        xs.append(-jnp.where(strict1, pq[:C] * dec, 0.0))
        attns.append(jnp.where(lower1, pq[C:] * dec, 0.0).astype(BF16))
    ts = [eye_f + x for x in xs]
    ps = [x.astype(BF16) for x in xs]
    for _ in range(5):
        ps = [_dot(p, p).astype(BF16) for p in ps]
        ts = [t + _dot(t.astype(BF16), p) for t, p in zip(ts, ps)]
    rs = []
    for x, t in zip(xs, ts):
        xh, xl = _split2(x)
        th, tl = _split2(t)
        rs.append((eye_f - t) + _dot_split(xh, xl, th, tl))
    ts = [t + _dot(t.astype(BF16), r.astype(BF16)) for t, r in zip(ts, rs)]
    sols = []
    for (g, c, rows), t in zip(chains, ts):
        th, tl = _split2(t)
        rh, rl = _split2(rhs_s[g, rows, :])
        sols.append(_dot_split(th, tl, rh, rl).astype(BF16))
    for (g, c, rows), attn, sol16 in zip(chains, attns, sols):
        aw = _dot(attn, sol16)
        o0_s[g, rows, :] = aw[:, :HEAD_DIM]
        qt_s[g, rows, :] = (qg_s[g, rows, :] - aw[:, HEAD_DIM:]).astype(BF16)
    for (g, c, rows), sol16 in zip(chains, sols):
        kwu = _dot_tn(kg_s[g, rows, :].astype(BF16), sol16)
        bb_s[g, c] = kwu[:, :HEAD_DIM]
        kw_s[g, c] = kwu[:, HEAD_DIM:].astype(BF16)

    gn = gn_ref[...]
    sts = [state_ref[g] for g in range(HG)]
    for c in range(NC):
        rows = slice(c * C, (c + 1) * C)
        st16 = [st.astype(BF16) for st in sts]
        sts = [sts[g] * egl_s[g, c * C:c * C + 1, :] - _dot(kw_s[g, c], st16[g]) + bb_s[g, c]
               for g in range(HG)]
        for g in range(HG):
            o = _dot(qt_s[g, rows, :], st16[g]) + o0_s[g, rows, :]
            on = o * lax.rsqrt(jnp.mean(o * o, axis=-1, keepdims=True) + NORM_EPS) * gn
            o_ref[rows, g * HEAD_DIM:(g + 1) * HEAD_DIM] = (on * _silu(z_ref[g, rows, :])).astype(o_ref.dtype)
    for g in range(HG):
        state_ref[g] = sts[g]


def _deltanet(dnz, ba, conv_w, a_log, dt_bias, gnorm, tb, hg):
    H = DN_HEADS
    S = dnz.shape[1]
    NC = tb // CHUNK
    nh = H // hg
    grid_spec = pltpu.PrefetchScalarGridSpec(
        num_scalar_prefetch=2,
        grid=(nh, S // tb),
        in_specs=[
            pl.BlockSpec((hg, tb, LANE), lambda h, t, *_: (h, t, 0)),
            pl.BlockSpec((hg, tb, LANE), lambda h, t, *_: (nh + h, t, 0)),
            pl.BlockSpec((hg, tb, LANE), lambda h, t, *_: (2 * nh + h, t, 0)),
            pl.BlockSpec((hg, tb, LANE), lambda h, t, *_: (3 * nh + h, t, 0)),
            pl.BlockSpec((hg, NC, CHUNK), lambda h, t, *_: (h, t, 0)),
            pl.BlockSpec((hg, NC, CHUNK), lambda h, t, *_: (nh + h, t, 0)),
            pl.BlockSpec((hg, 3, DN_CONV, LANE), lambda h, t, *_: (h, 0, 0, 0)),
            pl.BlockSpec((1, LANE), lambda h, t, *_: (0, 0)),
        ],
        out_specs=pl.BlockSpec((tb, hg * LANE), lambda h, t, *_: (t, h)),
        scratch_shapes=[
            pltpu.VMEM((hg, HEAD_DIM, HEAD_DIM), F32),
            pltpu.VMEM((hg, 3, 8, LANE), F32),
            pltpu.VMEM((tb + 8, LANE), F32),
            pltpu.VMEM((hg, tb, LANE), F32),
            pltpu.VMEM((hg, tb, LANE), F32),
            pltpu.VMEM((hg, tb, LANE), F32),
            pltpu.VMEM((hg, tb, 2 * LANE), F32),
            pltpu.VMEM((hg, tb, LANE), F32),
            pltpu.VMEM((hg, tb, LANE), F32),
            pltpu.VMEM((hg, tb, CHUNK), F32),
            pltpu.VMEM((hg, tb, LANE), F32),
            pltpu.VMEM((hg, tb, LANE), BF16),
            pltpu.VMEM((hg, tb, LANE), F32),
            pltpu.VMEM((hg, NC, HEAD_DIM, HEAD_DIM), BF16),
            pltpu.VMEM((hg, NC, HEAD_DIM, HEAD_DIM), F32),
        ],
    )
    return pl.pallas_call(
        _dn_kernel,
        grid_spec=grid_spec,
        out_shape=jax.ShapeDtypeStruct((S, H * HEAD_DIM), BF16),
        compiler_params=_cparams("parallel", "arbitrary"),
        name="deltanet",
    )(a_log, dt_bias, dnz, dnz, dnz, dnz, ba, ba, conv_w, gnorm)


SB_T = 128
SB_NEAR = 3


def _sb_kernel(q_ref, k_ref, v_ref, o_ref, acc_s, c_s):
    T = SB_T
    QB = q_ref.shape[1] // T
    qb0 = pl.program_id(1) * QB
    scale = HEAD_DIM ** -0.5
    rq = lax.broadcasted_iota(jnp.int32, (T, T), 0)
    ck = lax.broadcasted_iota(jnp.int32, (T, T), 1)
    diag_mask = ck < rq
    later_ones = jnp.concatenate([(rq > ck).astype(BF16), jnp.ones((T, T), BF16)], axis=1)
    later_ones2 = jnp.concatenate([later_ones, later_ones], axis=0)

    def tiles(qs, kbs, masks, cs):
        kbc = [jnp.maximum(kb, 0) for kb in kbs]
        starts = [pl.multiple_of(kb * T, T) for kb in kbc]
        zs = [_dot_nt(q, k_ref[0, pl.ds(s0, T), :]) * scale for q, s0 in zip(qs, starts)]
        lss, l1ms = [], []
        for z, m in zip(zs, masks):
            ls = jnp.minimum(z, 0.0) - jnp.log(1.0 + jnp.exp(-jnp.abs(z)))
            lss.append(ls)
            l1ms.append(jnp.where(m, ls - z, 0.0))
        sums = []
        for l1m in l1ms:
            sums.append(_dot(jnp.concatenate(_split2(l1m), axis=1), later_ones2))
        outs, new_cs = [], []
        for ls, m, sm, c, s0 in zip(lss, masks, sums, cs, starts):
            a = jnp.where(m, jnp.exp(ls + sm[:, :T] + c), 0.0)
            outs.append(_dot(a.astype(BF16), v_ref[0, pl.ds(s0, T), :]))
            new_cs.append(c + sm[:, T:])
        return outs, new_cs

    qs = [q_ref[0, i * T:(i + 1) * T, :] for i in range(QB)]
    cs = [jnp.zeros((T, T), F32) for _ in range(QB)]
    accs = [jnp.zeros((T, HEAD_DIM), F32) for _ in range(QB)]
    for j in range(SB_NEAR):
        kbs = [qb0 + i - j for i in range(QB)]
        masks = [diag_mask if j == 0 else jnp.broadcast_to(kb >= 0, (T, T)) for kb in kbs]
        outs, cs = tiles(qs, kbs, masks, cs)
        accs = [a + o for a, o in zip(accs, outs)]
    for i in range(QB):
        acc_s[i] = accs[i]
        c_s[i] = cs[i]

    def live_max(cs, t):
        return functools.reduce(jnp.maximum, [
            jnp.where(qb0 + i - SB_NEAR - t >= 0, jnp.max(c), -jnp.inf) for i, c in enumerate(cs)])

    def cond(carry):
        _, cm = carry
        return cm > SB_SKIP_LOG

    def body(carry):
        t, _ = carry
        kbs = [qb0 + i - SB_NEAR - t for i in range(QB)]
        masks = [jnp.broadcast_to(kb >= 0, (T, T)) for kb in kbs]
        outs, new_cs = tiles(qs, kbs, masks, [c_s[i] for i in range(QB)])
        for i in range(QB):
            acc_s[i] += outs[i]
            c_s[i] = new_cs[i]
        return t + 1, live_max(new_cs, t + 1)

    lax.while_loop(cond, body, (jnp.int32(0), live_max(cs, 0)))
    for i in range(QB):
        o_ref[i * T:(i + 1) * T, :] = acc_s[i].astype(o_ref.dtype)


def _stick_breaking(sb, tq):
    H = SB_HEADS
    S = sb.shape[1]
    return pl.pallas_call(
        _sb_kernel,
        grid=(H, S // tq),
        in_specs=[
            pl.BlockSpec((1, tq, LANE), lambda h, i: (h, i, 0)),
            pl.BlockSpec((1, S, LANE), lambda h, i: (H + h, 0, 0)),
            pl.BlockSpec((1, S, LANE), lambda h, i: (2 * H + h, 0, 0)),
        ],
        out_specs=pl.BlockSpec((tq, LANE), lambda h, i: (i, h)),
        out_shape=jax.ShapeDtypeStruct((S, H * HEAD_DIM), BF16),
        scratch_shapes=[pltpu.VMEM((tq // SB_T, SB_T, HEAD_DIM), F32),
                        pltpu.VMEM((tq // SB_T, SB_T, SB_T), F32)],
        compiler_params=_cparams("parallel", "arbitrary"),
        name="stick_breaking",
    )(sb, sb, sb)


def _merge_kernel(od_ref, os_ref, gd_ref, gs_ref, wd_ref, ws_ref, o_ref):
    m = (_sigmoid(gd_ref[...]) * _dot(od_ref[...], wd_ref[...])
         + _sigmoid(gs_ref[...]) * _dot(os_ref[...], ws_ref[...]))
    o_ref[...] = m.astype(o_ref.dtype)


def _merge(o_dn, o_sb, gates, w_dn, w_sb, tm, tn):
    S, W = o_dn.shape
    D = w_dn.shape[1]
    nb = D // tn
    return pl.pallas_call(
        _merge_kernel,
        grid=(nb, S // tm),
        in_specs=[
            pl.BlockSpec((tm, W), lambda j, i: (i, 0)),
            pl.BlockSpec((tm, W), lambda j, i: (i, 0)),
            pl.BlockSpec((tm, tn), lambda j, i: (i, j)),
            pl.BlockSpec((tm, tn), lambda j, i: (i, nb + j)),
            pl.BlockSpec((W, tn), lambda j, i: (0, j)),
            pl.BlockSpec((W, tn), lambda j, i: (0, j)),
        ],
        out_specs=pl.BlockSpec((tm, tn), lambda j, i: (i, j)),
        out_shape=jax.ShapeDtypeStruct((S, D), BF16),
        compiler_params=_cparams("parallel", "parallel"),
        name="merge",
    )(o_dn, o_sb, gates, gates, w_dn, w_sb)


def _outproj_kernel(m_ref, w_ref, h_ref, g_ref, o_ref):
    o_ref[...] = h_ref[...] + _rms(_dot(m_ref[...], w_ref[...]), g_ref[...])


def _outproj(m, w, h, g, tm):
    S, D = h.shape
    return pl.pallas_call(
        _outproj_kernel,
        grid=(S // tm,),
        in_specs=[
            pl.BlockSpec((tm, D), lambda i: (i, 0)),
            pl.BlockSpec((D, D), lambda i: (0, 0)),
            pl.BlockSpec((tm, D), lambda i: (i, 0)),
            pl.BlockSpec((1, D), lambda i: (0, 0)),
        ],
        out_specs=pl.BlockSpec((tm, D), lambda i: (i, 0)),
        out_shape=jax.ShapeDtypeStruct((S, D), F32),
        compiler_params=_cparams("parallel"),
        name="outproj",
    )(m, w, h, g)


def _ple_kernel(h_ref, p_ref, gpre_ref, wg_ref, wp_ref, gpost_ref, o_ref):
    h = h_ref[...]
    u = _rms(h, gpre_ref[...]).astype(BF16)
    gate = _sigmoid(_dot(u, wg_ref[...]))
    e = _dot(p_ref[...].astype(BF16), wp_ref[...])
    o_ref[...] = h + _rms(gate * e, gpost_ref[...])


def _ple(h, p, gpre, wg, wp, gpost, tm):
    S, D = h.shape
    P = p.shape[1]
    return pl.pallas_call(
        _ple_kernel,
        grid=(S // tm,),
        in_specs=[
            pl.BlockSpec((tm, D), lambda i: (i, 0)),
            pl.BlockSpec((tm, P), lambda i: (i, 0)),
            pl.BlockSpec((1, D), lambda i: (0, 0)),
            pl.BlockSpec((D, D), lambda i: (0, 0)),
            pl.BlockSpec((P, D), lambda i: (0, 0)),
            pl.BlockSpec((1, D), lambda i: (0, 0)),
        ],
        out_specs=pl.BlockSpec((tm, D), lambda i: (i, 0)),
        out_shape=jax.ShapeDtypeStruct((S, D), F32),
        compiler_params=_cparams("parallel"),
        name="ple",
    )(h, p, gpre, wg, wp, gpost)


def _layer(h, p, ffn1_norm_pre, ffn1_w_gate, ffn1_w_up, ffn1_w_down, ffn1_norm_post,
           mix_norm_pre, w_in, dn_conv_w, dn_A_log, dn_dt_bias, dn_out_norm,
           w_branch_dn, w_branch_sb, w_out, mix_norm_post,
           ffn2_norm_pre, ffn2_w_gate, ffn2_w_up, ffn2_w_down, ffn2_norm_post,
           ple_norm_pre, ple_w_gate, ple_w_proj, ple_norm_post):
    S, D = h.shape
    H = DN_HEADS
    dn_w = H * HEAD_DIM
    sb_w = SB_HEADS * HEAD_DIM
    row = lambda g: g.reshape(1, -1).astype(F32)
    bf = lambda w: w.astype(BF16)
    tm = min(512, S)

    h = _ffn(h, row(ffn1_norm_pre), bf(ffn1_w_gate), bf(ffn1_w_up), bf(ffn1_w_down),
             row(ffn1_norm_post), tm, 512)

    o2 = 4 * dn_w
    o4 = o2 + 2 * H
    o5 = o4 + 3 * sb_w
    gpre = row(mix_norm_pre)
    tmp = min(1024, S)
    w_bf = bf(w_in)
    w_b = w_bf[:, o4:]
    dnz = _proj_heads(h, gpre, w_bf, 0, o2, F32, tmp, 1024)
    sb = _proj_heads(h, gpre, w_b, 0, 3 * sb_w, BF16, tmp, 1024)
    gates, ba = _proj_rows(h, gpre, w_b, 3 * sb_w, 2 * D, bf(w_in[:, o2:o4].T), tmp, 1024)

    conv_w = dn_conv_w.reshape(DN_CONV, 3, H, HEAD_DIM).transpose(2, 1, 0, 3)
    o_dn = _deltanet(dnz, ba.reshape(2 * H, S // CHUNK, CHUNK), conv_w,
                     dn_A_log.astype(F32), dn_dt_bias.astype(F32), row(dn_out_norm), min(512, S), 4)
    o_sb = _stick_breaking(sb, min(4096, S))

    merged = _merge(o_dn, o_sb, gates, bf(w_branch_dn), bf(w_branch_sb), tm, 1024)
    h = _outproj(merged, bf(w_out), h, row(mix_norm_post), tm)

    h = _ffn(h, row(ffn2_norm_pre), bf(ffn2_w_gate), bf(ffn2_w_up), bf(ffn2_w_down),
             row(ffn2_norm_post), tm, 512)

    return _ple(h, p, row(ple_norm_pre), bf(ple_w_gate), bf(ple_w_proj), row(ple_norm_post), tm)


def kernel(x, p, ffn1_norm_pre, ffn1_w_gate, ffn1_w_up, ffn1_w_down, ffn1_norm_post, mix_norm_pre, w_in, dn_conv_w, dn_A_log, dn_dt_bias, dn_out_norm, w_branch_dn, w_branch_sb, w_out, mix_norm_post, ffn2_norm_pre, ffn2_w_gate, ffn2_w_up, ffn2_w_down, ffn2_norm_post, ple_norm_pre, ple_w_gate, ple_w_proj, ple_norm_post):
    B, S, D = x.shape
    depth = w_in.shape[0]
    outs = []
    for b in range(B):
        h = x[b]
        for i in range(depth):
            h = _layer(h, p[i, b], ffn1_norm_pre[i], ffn1_w_gate[i], ffn1_w_up[i], ffn1_w_down[i],
                       ffn1_norm_post[i], mix_norm_pre[i], w_in[i], dn_conv_w[i], dn_A_log[i],
                       dn_dt_bias[i], dn_out_norm[i], w_branch_dn[i], w_branch_sb[i], w_out[i],
                       mix_norm_post[i], ffn2_norm_pre[i], ffn2_w_gate[i], ffn2_w_up[i],
                       ffn2_w_down[i], ffn2_norm_post[i], ple_norm_pre[i], ple_w_gate[i],
                       ple_w_proj[i], ple_norm_post[i])
        outs.append(h)
    return outs[0].reshape(B, S, D) if B == 1 else jnp.stack(outs)
```
